```python
import jax, jax.numpy as jnp
from jax import lax
import numpy as np

D_MODEL = 1024
BATCH = 2
SEQ = 8192
DEPTH = 1
DEC_BATCH = 128
DEC_SEQ = 4
PAST_LEN = 2048
PAGE_SIZE = 128

N_HEADS = 8
HEAD_DIM = 64
ATT_DIM = N_HEADS * HEAD_DIM
IDX_HEADS = 8
IDX_DIM = 64
TOPK_MAX = 256
D_CONV = D_MODEL // 2
CONV_W = 31
D_FF = ((8 * D_MODEL // 3 + 255) // 256) * 256
ROPE_THETA = 10000.0
Q_BLOCK = 128
EPS = 1e-6
N_MOD = 9
NEG = -1e30
N_IN = 2 * D_CONV + 3 * ATT_DIM + IDX_HEADS * IDX_DIM + IDX_DIM + IDX_HEADS + 2 * D_MODEL

kernel_name = 'macaron_conv_dsa_hybrid_step'


def rmsnorm(x, g):
    xf = x.astype(jnp.float32)
    y = xf * lax.rsqrt(jnp.mean(xf * xf, axis=-1, keepdims=True) + EPS)
    return (y * g.astype(jnp.float32)).astype(x.dtype)


def layernorm(x, g, b):
    xf = x.astype(jnp.float32)
    mu = jnp.mean(xf, axis=-1, keepdims=True)
    var = jnp.mean(jnp.square(xf - mu), axis=-1, keepdims=True)
    y = (xf - mu) * lax.rsqrt(var + EPS)
    return (y * g.astype(jnp.float32) + b.astype(jnp.float32)).astype(x.dtype)


def rope_tables(pos, dim):
    inv = ROPE_THETA ** (-jnp.arange(0, dim, 2, dtype=jnp.float32) / dim)
    ang = pos.astype(jnp.float32)[:, None] * inv[None, :]
    return jnp.cos(ang), jnp.sin(ang)


def apply_rope(x, cos, sin):
    c = cos[None, :, None, :].astype(x.dtype)
    s = sin[None, :, None, :].astype(x.dtype)
    x1, x2 = jnp.split(x, 2, axis=-1)
    return jnp.concatenate([x1 * c - x2 * s, x2 * c + x1 * s], axis=-1)


def ada(c, w_ada, b_ada):
    m = jax.nn.silu(c) @ w_ada + b_ada
    m = m.reshape(c.shape[0], 1, N_MOD, D_MODEL)
    return tuple(m[:, :, i] for i in range(N_MOD))


def modulate(x, g, shift, scale):
    return rmsnorm(x, g) * (1.0 + scale) + shift


def swiglu(h, wg, wu, wd):
    return (jax.nn.silu(h @ wg) * (h @ wu)) @ wd


def split_in(n, w_in, cos_h, sin_h, cos_i, sin_i):
    B, T = n.shape[0], n.shape[1]
    z = n @ w_in
    sizes = [D_CONV, D_CONV, ATT_DIM, ATT_DIM, ATT_DIM, IDX_HEADS * IDX_DIM, IDX_DIM, IDX_HEADS, D_MODEL, D_MODEL]
    cuts = [int(s) for s in np.cumsum(sizes)[:-1]]
    a_lin, a_gate, q, k, v, qi, ki, wi, g_c, g_a = jnp.split(z, cuts, axis=-1)
    u = a_lin * jax.nn.sigmoid(a_gate)
    q = apply_rope(q.reshape(B, T, N_HEADS, HEAD_DIM), cos_h, sin_h)
    k = apply_rope(k.reshape(B, T, N_HEADS, HEAD_DIM), cos_h, sin_h)
    v = v.reshape(B, T, N_HEADS, HEAD_DIM)
    qi = apply_rope(qi.reshape(B, T, IDX_HEADS, IDX_DIM), cos_i, sin_i)
    ki = apply_rope(ki.reshape(B, T, 1, IDX_DIM), cos_i, sin_i)[:, :, 0]
    wi = wi * (IDX_HEADS * IDX_DIM) ** -0.5
    return u, q, k, v, qi, ki, wi, g_c, g_a


def conv_branch(u_ext, w_dw, b_dw, ln_g, ln_b, w_conv_out):
    kern = w_dw[:, None, :].astype(u_ext.dtype)
    y = lax.conv_general_dilated(u_ext, kern, window_strides=(1,), padding='VALID',
                                 dimension_numbers=('NWC', 'WIO', 'NWC'),
                                 feature_group_count=D_CONV) + b_dw
    y = jax.nn.silu(layernorm(y, ln_g, ln_b))
    return y @ w_conv_out


def index_scores(qi, ki, wi):
    s = jnp.einsum('bqhd,bld->bqhl', qi.astype(jnp.float32), ki.astype(jnp.float32))
    return jnp.einsum('bqhl,bqh->bql', jax.nn.relu(s), wi.astype(jnp.float32))


def sparse_attend(q, kg, vg, valid):
    logits = jnp.einsum('bqhd,bqkhd->bhqk', q.astype(jnp.float32), kg.astype(jnp.float32)) * HEAD_DIM ** -0.5
    logits = jnp.where(valid[:, None], logits, NEG)
    p = jax.nn.softmax(logits, axis=-1).astype(vg.dtype)
    return jnp.einsum('bhqk,bqkhd->bqhd', p, vg)


def take_rows(a, idx):
    return jax.vmap(lambda aa, ii: aa[ii])(a, idx)


def dsa_prompt(q, k, v, qi, ki, wi):
    B, T = q.shape[0], q.shape[1]
    topk = min(TOPK_MAX, T // 4)
    nb = T // Q_BLOCK
    key_pos = jnp.arange(T)

    def to_blocks(a):
        return a.reshape((B, nb, Q_BLOCK) + a.shape[2:]).swapaxes(0, 1)

    def block(args):
        bi, qb, qib, wib = args
        qpos = bi * Q_BLOCK + jnp.arange(Q_BLOCK)
        s = index_scores(qib, ki, wib)
        s = jnp.where(key_pos[None, None, :] <= qpos[None, :, None], s, NEG)
        _, idx = lax.top_k(s, topk)
        valid = idx <= qpos[None, :, None]
        return sparse_attend(qb, take_rows(k, idx), take_rows(v, idx), valid)

    out = lax.map(block, (jnp.arange(nb), to_blocks(q), to_blocks(qi), to_blocks(wi)))
    return out.swapaxes(0, 1).reshape(B, T, N_HEADS, HEAD_DIM)


def dsa_sample(q, k_new, v_new, qi, ki_new, wi, cache_k, cache_v, cache_idx_k, page_table):
    DB, S = q.shape[0], q.shape[1]
    past = page_table.shape[1] * PAGE_SIZE
    L = past + S
    topk = min(TOPK_MAX, L // 4)
    ki_past = cache_idx_k[page_table].reshape(DB, past, IDX_DIM).astype(ki_new.dtype)
    ki_all = jnp.concatenate([ki_past, ki_new], axis=1)
    qpos = past + jnp.arange(S)
    key_pos = jnp.arange(L)
    s = index_scores(qi, ki_all, wi)
    s = jnp.where(key_pos[None, None, :] <= qpos[None, :, None], s, NEG)
    _, idx = lax.top_k(s, topk)
    idx_p = jnp.minimum(idx, past - 1)
    phys = take_rows(page_table, idx_p // PAGE_SIZE)
    slot = idx_p % PAGE_SIZE
    kp = cache_k[phys, slot].astype(k_new.dtype)
    vp = cache_v[phys, slot].astype(v_new.dtype)
    idx_n = jnp.clip(idx - past, 0, S - 1)
    is_past = (idx < past)[..., None, None]
    kg = jnp.where(is_past, kp, take_rows(k_new, idx_n))
    vg = jnp.where(is_past, vp, take_rows(v_new, idx_n))
    valid = idx <= qpos[None, :, None]
    return sparse_attend(q, kg, vg, valid)


def trunk_layer(x, c, pos, conv_prefix, attend, lp):
    B, T = x.shape[0], x.shape[1]
    s1, sc1, g1, s2, sc2, g2, s3, sc3, g3 = ada(c, lp['w_ada'], lp['b_ada'])
    h = x + 0.5 * g1 * swiglu(modulate(x, lp['g_ffn1'], s1, sc1), lp['w1_gate'], lp['w1_up'], lp['w1_down'])
    n = modulate(h, lp['g_mix'], s2, sc2)
    cos_h, sin_h = rope_tables(pos, HEAD_DIM)
    cos_i, sin_i = rope_tables(pos, IDX_DIM)
    u, q, k, v, qi, ki, wi, g_c, g_a = split_in(n, lp['w_in'], cos_h, sin_h, cos_i, sin_i)
    u_ext = jnp.concatenate([conv_prefix.astype(u.dtype), u], axis=1)
    conv_out = conv_branch(u_ext, lp['w_dw'], lp['b_dw'], lp['ln_g'], lp['ln_b'], lp['w_conv_out'])
    att = attend(q, k, v, qi, ki, wi).reshape(B, T, ATT_DIM) @ lp['w_attn_o']
    mix = (jax.nn.sigmoid(g_c) * conv_out + jax.nn.sigmoid(g_a) * att) @ lp['w_out']
    h = h + g2 * mix
    h = h + 0.5 * g3 * swiglu(modulate(h, lp['g_ffn2'], s3, sc3), lp['w2_gate'], lp['w2_up'], lp['w2_down'])
    new_conv = u_ext[:, -(CONV_W - 1):]
    return h, k, v, ki, new_conv


def setup_inputs(seed: int = 0) -> dict:
    key = jax.random.key(seed)
    ks = iter(jax.random.split(key, 40))
    nrm = lambda shape, scale: jax.random.normal(next(ks), shape, jnp.float32) * scale
    n_pages = PAST_LEN // PAGE_SIZE
    n_pool = (DEC_BATCH * n_pages * 5) // 4
    perm = jax.random.permutation(next(ks), n_pool)[: DEC_BATCH * n_pages]
    page_table = perm.reshape(DEC_BATCH, n_pages).astype(jnp.int32)
    gain = lambda shape: 1.0 + nrm(shape, 0.1)
    return {
        'x_prompt': nrm((BATCH, SEQ, D_MODEL), 1.0),
        'x_sample': nrm((DEC_BATCH, DEC_SEQ, D_MODEL), 1.0),
        'cache_k': nrm((DEPTH, n_pool, PAGE_SIZE, N_HEADS, HEAD_DIM), 1.0),
        'cache_v': nrm((DEPTH, n_pool, PAGE_SIZE, N_HEADS, HEAD_DIM), 1.0),
        'cache_idx_k': nrm((DEPTH, n_pool, PAGE_SIZE, IDX_DIM), 1.0),
        'state_conv': nrm((DEPTH, DEC_BATCH, CONV_W - 1, D_CONV), 0.5),
        'page_table': page_table,
        'c_prompt': nrm((BATCH, D_MODEL), 1.0),
        'c_sample': nrm((DEC_BATCH, D_MODEL), 1.0),
        'w_ada': nrm((DEPTH, D_MODEL, N_MOD * D_MODEL), D_MODEL ** -0.5),
        'b_ada': nrm((DEPTH, N_MOD * D_MODEL), 0.01),
        'g_ffn1': gain((DEPTH, D_MODEL)),
        'w1_gate': nrm((DEPTH, D_MODEL, D_FF), D_MODEL ** -0.5),
        'w1_up': nrm((DEPTH, D_MODEL, D_FF), D_MODEL ** -0.5),
        'w1_down': nrm((DEPTH, D_FF, D_MODEL), D_FF ** -0.5),
        'g_mix': gain((DEPTH, D_MODEL)),
        'w_in': nrm((DEPTH, D_MODEL, N_IN), D_MODEL ** -0.5),
        'w_dw': nrm((DEPTH, CONV_W, D_CONV), CONV_W ** -0.5),
        'b_dw': nrm((DEPTH, D_CONV), 0.01),
        'ln_g': gain((DEPTH, D_CONV)),
        'ln_b': nrm((DEPTH, D_CONV), 0.01),
        'w_conv_out': nrm((DEPTH, D_CONV, D_MODEL), D_CONV ** -0.5),
        'w_attn_o': nrm((DEPTH, ATT_DIM, D_MODEL), ATT_DIM ** -0.5),
        'w_out': nrm((DEPTH, D_MODEL, D_MODEL), D_MODEL ** -0.5),
        'g_ffn2': gain((DEPTH, D_MODEL)),
        'w2_gate': nrm((DEPTH, D_MODEL, D_FF), D_MODEL ** -0.5),
        'w2_up': nrm((DEPTH, D_MODEL, D_FF), D_MODEL ** -0.5),
        'w2_down': nrm((DEPTH, D_FF, D_MODEL), D_FF ** -0.5),
        'g_final': gain((D_MODEL,)),
    }


def reference(x_prompt, x_sample, cache_k, cache_v, cache_idx_k, state_conv, page_table, c_prompt, c_sample,
              w_ada, b_ada, g_ffn1, w1_gate, w1_up, w1_down, g_mix, w_in, w_dw, b_dw, ln_g, ln_b,
              w_conv_out, w_attn_o, w_out, g_ffn2, w2_gate, w2_up, w2_down, g_final):
    B, T = x_prompt.shape[0], x_prompt.shape[1]
    DB, S = x_sample.shape[0], x_sample.shape[1]
    past = page_table.shape[1] * PAGE_SIZE
    pos_p = jnp.arange(T)
    pos_s = past + jnp.arange(S)
    hp, hs = x_prompt, x_sample
    kp_l, vp_l, ip_l, cp_l, ks_l, vs_l, is_l, cs_l = [], [], [], [], [], [], [], []
    for l in range(DEPTH):
        lp = {'w_ada': w_ada[l], 'b_ada': b_ada[l], 'g_ffn1': g_ffn1[l], 'w1_gate': w1_gate[l],
              'w1_up': w1_up[l], 'w1_down': w1_down[l], 'g_mix': g_mix[l], 'w_in': w_in[l],
              'w_dw': w_dw[l], 'b_dw': b_dw[l], 'ln_g': ln_g[l], 'ln_b': ln_b[l],
              'w_conv_out': w_conv_out[l], 'w_attn_o': w_attn_o[l], 'w_out': w_out[l],
              'g_ffn2': g_ffn2[l], 'w2_gate': w2_gate[l], 'w2_up': w2_up[l], 'w2_down': w2_down[l]}
        zero_prefix = jnp.zeros((B, CONV_W - 1, D_CONV), hp.dtype)
        hp, kp, vp, ip, cp = trunk_layer(hp, c_prompt, pos_p, zero_prefix, dsa_prompt, lp)
        ck, cv, ci = cache_k[l], cache_v[l], cache_idx_k[l]
        attend_s = lambda q, k, v, qi, ki, wi: dsa_sample(q, k, v, qi, ki, wi, ck, cv, ci, page_table)
        hs, ks_, vs_, is_, cs = trunk_layer(hs, c_sample, pos_s, state_conv[l], attend_s, lp)
        kp_l.append(kp); vp_l.append(vp); ip_l.append(ip); cp_l.append(cp)
        ks_l.append(ks_); vs_l.append(vs_); is_l.append(is_); cs_l.append(cs)
    y_prompt = rmsnorm(hp, g_final)
    y_sample = rmsnorm(hs, g_final)
    return (y_prompt, y_sample,
            jnp.stack(kp_l), jnp.stack(vp_l), jnp.stack(ip_l), jnp.stack(cp_l),
            jnp.stack(ks_l), jnp.stack(vs_l), jnp.stack(is_l), jnp.stack(cs_l))
```

```python
import functools

import jax
import jax.numpy as jnp
from jax import lax
from jax.experimental import pallas as pl
from jax.experimental.pallas import tpu as pltpu

F32 = jnp.float32
BF16 = jnp.bfloat16
I32 = jnp.int32

D_MODEL = 1024
N_HEADS = 8
HEAD_DIM = 64
ATT_DIM = N_HEADS * HEAD_DIM
IDX_HEADS = 8
IDX_DIM = 64
D_CONV = D_MODEL // 2
CONV_W = 31
D_FF = ((8 * D_MODEL // 3 + 255) // 256) * 256
N_MOD = 9
TOPK_MAX = 256
PAGE_SIZE = 128
ROPE_THETA = 10000.0
EPS = 1e-6
NEG = -1e30

LANES = 128
FF_CHUNK = 256
N_FF_CHUNKS = D_FF // FF_CHUNK
TOKEN_TILE = 512
CONV_HALO = 32
CONV_ROWS = 64
DSA_TQ = 128
DSA_TK = 256
DSA_RG = 64
VMEM_LIMIT = 56 * 1024 * 1024
INT_MIN = -2 ** 31
INT_MAX = 2 ** 31 - 1
KEY_NEG_INF = INT_MIN + 2 ** 23 - 1


def _dot(a, b):
    return jnp.dot(a, b, preferred_element_type=F32)


def _dot_nt(a, b):
    return lax.dot_general(a, b, (((1,), (1,)), ((), ())), preferred_element_type=F32)


def _sigmoid(x):
    return jax.nn.sigmoid(x)


def _rms_mod(x, g, scale, shift):
    ms = jnp.mean(x * x, axis=-1, keepdims=True)
    return (x * lax.rsqrt(ms + EPS)) * g * (1.0 + scale) + shift


def _key_to_f32(key):
    return pltpu.bitcast(key ^ ((key >> 31) & INT_MAX), F32)


def _const_spec(shape):
    nd = len(shape)
    return pl.BlockSpec(shape, lambda *_: (0,) * nd, pipeline_mode=pl.Buffered(1))


def _params(n_grid):
    return pltpu.CompilerParams(dimension_semantics=("arbitrary",) * n_grid, vmem_limit_bytes=VMEM_LIMIT)


def _ada_kernel(c_ref, w_ref, b_ref, o_ref):
    c = c_ref[...]
    s = (c * _sigmoid(c)).astype(BF16)
    o_ref[...] = _dot(s, w_ref[...].astype(BF16)) + b_ref[...]


def _ada(c, w, b):
    rows = c.shape[0]
    nb = D_MODEL
    return pl.pallas_call(
        _ada_kernel,
        out_shape=jax.ShapeDtypeStruct((rows, N_MOD * D_MODEL), F32),
        grid=(N_MOD * D_MODEL // nb,),
        in_specs=[pl.BlockSpec((rows, D_MODEL), lambda j: (0, 0)),
                  pl.BlockSpec((D_MODEL, nb), lambda j: (0, j)),
                  pl.BlockSpec((1, nb), lambda j: (0, j))],
        out_specs=pl.BlockSpec((rows, nb), lambda j: (0, j)),
        compiler_params=_params(1),
        name="ada",
    )(c, w, b.reshape(1, -1))


def _swiglu_into(hn_ref, wg_ref, wu_ref, wd_ref, acc_ref):
    acc_ref[...] = jnp.zeros_like(acc_ref)

    def body(i, carry):
        hn = hn_ref[...]
        a = _dot(hn, wg_ref[i])
        b = _dot(hn, wu_ref[i])
        act = (a * _sigmoid(a) * b).astype(BF16)
        acc_ref[...] += _dot(act, wd_ref[i])
        return carry

    lax.fori_loop(0, N_FF_CHUNKS, body, 0)


def _ffn_kernel(x_ref, sh_ref, sc_ref, gt_ref, g_ref, wg_ref, wu_ref, wd_ref, o_ref, hn_ref, acc_ref):
    x = x_ref[0]
    hn_ref[...] = _rms_mod(x, g_ref[...], sc_ref[0], sh_ref[0]).astype(BF16)
    _swiglu_into(hn_ref, wg_ref, wu_ref, wd_ref, acc_ref)
    o_ref[0] = x + 0.5 * gt_ref[0] * acc_ref[...]


def _tile_spec(tm, width, per_token):
    if per_token:
        return pl.BlockSpec((1, tm, width), lambda b, i: (b, i, 0))
    return pl.BlockSpec((1, 1, width), lambda b, i: (b, 0, 0))


def _ffn(x, shift, scale, gate, g, wg, wu, wd):
    s, t, _ = x.shape
    tm = min(TOKEN_TILE, t)
    per_token = shift.shape[1] != 1
    tok = _tile_spec(tm, D_MODEL, True)
    mod = _tile_spec(tm, D_MODEL, per_token)
    return pl.pallas_call(
        _ffn_kernel,
        out_shape=jax.ShapeDtypeStruct(x.shape, F32),
        grid=(s, t // tm),
        in_specs=[tok, mod, mod, mod, _const_spec((1, D_MODEL)),
                  _const_spec(wg.shape), _const_spec(wu.shape), _const_spec(wd.shape)],
        out_specs=tok,
        scratch_shapes=[pltpu.VMEM((tm, D_MODEL), BF16), pltpu.VMEM((tm, D_MODEL), F32)],
        compiler_params=_params(2),
        name="ffn1",
    )(x, shift, scale, gate, g, wg, wu, wd)


def _rope(x, cos, sin, lo32):
    sw = jnp.where(lo32, pltpu.roll(x, 96, 1), pltpu.roll(x, 32, 1))
    return x * cos + sw * sin


def _inproj_kernel(h_ref, sh_ref, sc_ref, g_ref, cos_ref, sin_ref,
                   wglu_ref, wqk_ref, wv_ref, wqi_ref, wkw_ref, wgt_ref,
                   u_ref, q_ref, k_ref, kb_ref, v_ref, vb_ref, qi_ref, kid_ref, ki_ref, wi_ref, sgc_ref, sga_ref,
                   n_ref):
    n_ref[...] = _rms_mod(h_ref[0], g_ref[...], sc_ref[0], sh_ref[0]).astype(BF16)
    cos = cos_ref[...]
    sin = sin_ref[...]
    lane = lax.broadcasted_iota(I32, (1, LANES), 1)
    lo32 = (lane & (HEAD_DIM - 1)) < HEAD_DIM // 2
    rope = functools.partial(_rope, cos=cos, sin=sin, lo32=lo32)

    z = _dot(n_ref[...], wglu_ref[...])
    u_ref[0] = z[:, :D_CONV] * _sigmoid(z[:, D_CONV:])

    z = _dot(n_ref[...], wqk_ref[...])
    for s in range(ATT_DIM // LANES):
        sl = slice(s * LANES, (s + 1) * LANES)
        q_ref[0, :, sl] = (rope(z[:, sl]) * HEAD_DIM ** -0.5).astype(BF16)
        xk = rope(z[:, ATT_DIM + s * LANES:ATT_DIM + (s + 1) * LANES])
        k_ref[0, :, sl] = xk
        kb_ref[0, :, sl] = xk.astype(BF16)

    z = _dot(n_ref[...], wv_ref[...])
    v_ref[0] = z
    vb_ref[0] = z.astype(BF16)

    z = _dot(n_ref[...], wqi_ref[...])
    for s in range(IDX_HEADS * IDX_DIM // LANES):
        sl = slice(s * LANES, (s + 1) * LANES)
        qi_ref[0, :, sl] = rope(z[:, sl]).astype(BF16)

    z = _dot(n_ref[...], wkw_ref[...])
    kid = rope(z[:, :LANES])
    kid_ref[0] = kid.astype(BF16)
    ki_ref[0] = kid[:, :IDX_DIM]
    wi_ref[0] = z[:, LANES:LANES + IDX_HEADS] * (IDX_HEADS * IDX_DIM) ** -0.5

    z = _dot(n_ref[...], wgt_ref[...])
    sgc_ref[0] = _sigmoid(z[:, :D_MODEL]).astype(BF16)
    sga_ref[0] = _sigmoid(z[:, D_MODEL:]).astype(BF16)


def _inproj(h, shift, scale, g, cos, sin, ws):
    s, t, _ = h.shape
    tm = min(TOKEN_TILE, t)
    per_token = shift.shape[1] != 1
    mod = _tile_spec(tm, D_MODEL, per_token)
    tab = pl.BlockSpec((tm, LANES), lambda b, i: (i, 0))
    widths = [(D_CONV, F32), (ATT_DIM, BF16), (ATT_DIM, F32), (ATT_DIM, BF16), (ATT_DIM, F32), (ATT_DIM, BF16),
              (IDX_HEADS * IDX_DIM, BF16), (LANES, BF16), (IDX_DIM, F32), (IDX_HEADS, F32),
              (D_MODEL, BF16), (D_MODEL, BF16)]
    return pl.pallas_call(
        _inproj_kernel,
        out_shape=[jax.ShapeDtypeStruct((s, t, w), dt) for w, dt in widths],
        grid=(s, t // tm),
        in_specs=[_tile_spec(tm, D_MODEL, True), mod, mod, _const_spec((1, D_MODEL)), tab, tab]
                 + [_const_spec(w.shape) for w in ws],
        out_specs=[_tile_spec(tm, w, True) for w, _ in widths],
        scratch_shapes=[pltpu.VMEM((tm, D_MODEL), BF16)],
        compiler_params=_params(2),
        name="inproj",
    )(h, shift, scale, g, cos, sin, *ws)


def _ln_silu(y, g, b):
    mu = jnp.mean(y, axis=-1, keepdims=True)
    yc = y - mu
    var = jnp.mean(yc * yc, axis=-1, keepdims=True)
    y = yc * lax.rsqrt(var + EPS) * g + b
    return y * _sigmoid(y)


def _conv_kernel(u_ref, up_ref, wdw_ref, bdw_ref, lng_ref, lnb_ref, wco_ref, o_ref, ext_ref, y_ref, *, tm):
    i = pl.program_id(1)
    ext_ref[0:CONV_HALO, :] = jnp.where(i > 0, up_ref[0], 0.0)
    ext_ref[CONV_HALO:, :] = u_ref[0]
    off = CONV_HALO - (CONV_W - 1)
    for r in range(tm // CONV_ROWS):
        acc = jnp.broadcast_to(bdw_ref[...], (CONV_ROWS, D_CONV))
        for j in range(CONV_W):
            lo = r * CONV_ROWS + off + j
            acc = acc + wdw_ref[j:j + 1, :] * ext_ref[lo:lo + CONV_ROWS, :]
        y_ref[r * CONV_ROWS:(r + 1) * CONV_ROWS, :] = _ln_silu(acc, lng_ref[...], lnb_ref[...]).astype(BF16)
    o_ref[0] = _dot(y_ref[...], wco_ref[...]).astype(BF16)


def _conv_prompt(u, wdw, bdw, lng, lnb, wco):
    b, t, _ = u.shape
    tm = min(TOKEN_TILE, t)
    halo_blocks = tm // CONV_HALO
    return pl.pallas_call(
        functools.partial(_conv_kernel, tm=tm),
        out_shape=jax.ShapeDtypeStruct((b, t, D_MODEL), BF16),
        grid=(b, t // tm),
        in_specs=[pl.BlockSpec((1, tm, D_CONV), lambda s, i: (s, i, 0)),
                  pl.BlockSpec((1, CONV_HALO, D_CONV), lambda s, i: (s, jnp.maximum(i * halo_blocks - 1, 0), 0)),
                  _const_spec(wdw.shape), _const_spec(bdw.shape), _const_spec(lng.shape), _const_spec(lnb.shape),
                  _const_spec(wco.shape)],
        out_specs=pl.BlockSpec((1, tm, D_MODEL), lambda s, i: (s, i, 0)),
        scratch_shapes=[pltpu.VMEM((tm + CONV_HALO, D_CONV), F32), pltpu.VMEM((tm, D_CONV), BF16)],
        compiler_params=_params(2),
        name="conv_prompt",
    )(u, u, wdw, bdw, lng, lnb, wco)


def _conv_sample_kernel(ext_ref, wdw_ref, bdw_ref, lng_ref, lnb_ref, wco_ref, o_ref, *, steps, rows):
    for s in range(steps):
        acc = jnp.broadcast_to(bdw_ref[...], (rows, D_CONV))
        for j in range(CONV_W):
            acc = acc + wdw_ref[j:j + 1, :] * ext_ref[s + j]
        y = _ln_silu(acc, lng_ref[...], lnb_ref[...]).astype(BF16)
        o_ref[s] = _dot(y, wco_ref[...]).astype(BF16)


def _conv_sample(ext_t, wdw, bdw, lng, lnb, wco):
    n, rows, _ = ext_t.shape
    steps = n - (CONV_W - 1)
    return pl.pallas_call(
        functools.partial(_conv_sample_kernel, steps=steps, rows=rows),
        out_shape=jax.ShapeDtypeStruct((steps, rows, D_MODEL), BF16),
        grid=(1,),
        in_specs=[_const_spec(ext_t.shape), _const_spec(wdw.shape), _const_spec(bdw.shape),
                  _const_spec(lng.shape), _const_spec(lnb.shape), _const_spec(wco.shape)],
        out_specs=pl.BlockSpec((steps, rows, D_MODEL), lambda i: (0, 0, 0)),
        compiler_params=_params(1),
        name="conv_sample",
    )(ext_t, wdw, bdw, lng, lnb, wco)


def _select_threshold(count, rows):
    def bit_body(b, key):
        cand = key + lax.shift_left(jnp.int32(1), 31 - b)
        cand_f = _key_to_f32(cand)
        return jnp.where(count(lambda s, kpos: jnp.where(s >= cand_f, 1, 0)) >= TOPK_MAX, cand, key)

    key = lax.fori_loop(0, 32, bit_body, jnp.full((rows, 1), INT_MIN, I32))
    thr = jnp.where(key < KEY_NEG_INF, -jnp.inf, _key_to_f32(key))
    need = TOPK_MAX - count(lambda s, kpos: jnp.where(s > thr, 1, 0))
    n_eq = count(lambda s, kpos: jnp.where(s == thr, 1, 0))
    return thr, need, n_eq


def _tie_position(count, thr, need, n_bits):
    def bit_body(b, j):
        cand = j + lax.shift_left(jnp.int32(1), n_bits - 1 - b)
        before = count(lambda s, kpos: jnp.where(s == thr, jnp.where(kpos < cand, 1, 0), 0))
        return jnp.where(before < need, cand, j)

    return lax.fori_loop(0, n_bits, bit_body, jnp.zeros(thr.shape, I32))


def _bias_from_scores(scores, kpos, qpos, thr, last_tie):
    t1 = jnp.where(kpos <= last_tie, 0.0, NEG)
    t2 = jnp.where(scores == thr, t1, NEG)
    t3 = jnp.where(scores > thr, 0.0, t2)
    return jnp.where(kpos <= qpos, t3, NEG)


def _split_heads(x, lo):
    zero = jnp.zeros_like(x)
    return jnp.where(lo, x, zero), jnp.where(lo, zero, x)


def _dsa_prompt_kernel(qi_ref, wi_ref, q_ref, kid_ref, kb_ref, vb_ref, o_ref,
                       qim_ref, qm_ref, wib_ref, sc_ref, tie_ref, m_ref, l_ref, acc_ref, *, tq, tk, n_bits):
    q0 = pl.program_id(1) * tq
    n_ch = (q0 + tq + tk - 1) // tk
    lo = lax.broadcasted_iota(I32, (tq, LANES), 1) < HEAD_DIM

    for pr in range(N_HEADS // 2):
        sl = slice(pr * LANES, (pr + 1) * LANES)
        qim_ref[2 * pr], qim_ref[2 * pr + 1] = _split_heads(qi_ref[0, :, sl], lo)
        qm_ref[2 * pr], qm_ref[2 * pr + 1] = _split_heads(q_ref[0, :, sl], lo)
    for h in range(IDX_HEADS):
        wib_ref[h] = jnp.broadcast_to(wi_ref[0, :, h:h + 1], (tq, tk))

    row = lax.broadcasted_iota(I32, (tq, tk), 0) + q0
    col = lax.broadcasted_iota(I32, (tq, tk), 1)

    def score_chunk(c, carry):
        k0 = pl.multiple_of(c * tk, tk)
        kc = kid_ref[0, pl.ds(k0, tk), :]
        acc = jnp.zeros((tq, tk), F32)
        for h in range(IDX_HEADS):
            acc = acc + wib_ref[h] * jnp.maximum(_dot_nt(qim_ref[h], kc), 0.0)
        sc_ref[c] = jnp.where(col + k0 <= row, acc, NEG)
        return carry

    lax.fori_loop(0, n_ch, score_chunk, 0)

    rg = DSA_RG
    row_g = lax.broadcasted_iota(I32, (rg, tk), 0) + q0
    col_g = lax.broadcasted_iota(I32, (rg, tk), 1)
    for g in range(tq // rg):
        r0 = g * rg

        def count(pred, r0=r0):
            def body(c, acc):
                return acc + pred(sc_ref[c, r0:r0 + rg, :], col_g + c * tk)

            acc = lax.fori_loop(0, n_ch, body, jnp.zeros((rg, tk), I32))
            return jnp.sum(acc.astype(F32), axis=1, keepdims=True)

        thr, need, n_eq = _select_threshold(count, rg)
        tie_ref[r0:r0 + rg, :] = jnp.full((rg, 1), INT_MAX, I32)
        split = n_eq > need

        @pl.when(jnp.max(jnp.where(split, 1.0, 0.0)) > 0.5)
        def _():
            j = _tie_position(count, thr, need, n_bits)
            tie_ref[r0:r0 + rg, :] = jnp.where(split, j, INT_MAX)

        last_tie = tie_ref[r0:r0 + rg, :]

        def to_bias(c, carry, r0=r0, thr=thr, last_tie=last_tie):
            sc = sc_ref[c, r0:r0 + rg, :]
            sc_ref[c, r0:r0 + rg, :] = _bias_from_scores(sc, col_g + c * tk, row_g + r0, thr, last_tie)
            return carry

        lax.fori_loop(0, n_ch, to_bias, 0)

    m_ref[...] = jnp.full(m_ref.shape, NEG, F32)
    l_ref[...] = jnp.zeros(l_ref.shape, F32)
    acc_ref[...] = jnp.zeros(acc_ref.shape, F32)

    def attend_chunk(c, carry):
        k0 = pl.multiple_of(c * tk, tk)
        bias = sc_ref[c]
        for pr in range(N_HEADS // 2):
            sl = slice(pr * LANES, (pr + 1) * LANES)
            kc = kb_ref[0, pl.ds(k0, tk), sl]
            vc = vb_ref[0, pl.ds(k0, tk), sl]
            upd = []
            for e in range(2):
                h = 2 * pr + e
                s = _dot_nt(qm_ref[h], kc) + bias
                m_prev = m_ref[h]
                m_new = jnp.maximum(m_prev, jnp.max(s, axis=1, keepdims=True))
                alpha = jnp.exp(m_prev - m_new)
                p = jnp.concatenate([jnp.exp(s[:, t * LANES:(t + 1) * LANES] - m_new) for t in range(tk // LANES)],
                                    axis=1)
                l_ref[h] = alpha * l_ref[h] + jnp.sum(p, axis=1, keepdims=True)
                m_ref[h] = m_new
                upd.append(acc_ref[pr] * alpha + _dot(p.astype(BF16), vc))
            acc_ref[pr] = jnp.where(lo, upd[0], upd[1])
        return carry

    lax.fori_loop(0, n_ch, attend_chunk, 0)

    for pr in range(N_HEADS // 2):
        inv = jnp.where(lo, 1.0 / l_ref[2 * pr], 1.0 / l_ref[2 * pr + 1])
        o_ref[0, :, pr * LANES:(pr + 1) * LANES] = (acc_ref[pr] * inv).astype(BF16)


def _dsa_prompt(qi, wi, q, kid, kb, vb):
    b, t, _ = q.shape
    tq, tk = DSA_TQ, DSA_TK
    n_bits = max(1, (t - 1).bit_length())
    qspec = lambda w: pl.BlockSpec((1, tq, w), lambda s, i: (s, i, 0))
    kspec = lambda w: pl.BlockSpec((1, t, w), lambda s, i: (s, 0, 0), pipeline_mode=pl.Buffered(1))
    return pl.pallas_call(
        functools.partial(_dsa_prompt_kernel, tq=tq, tk=tk, n_bits=n_bits),
        out_shape=jax.ShapeDtypeStruct((b, t, ATT_DIM), BF16),
        grid=(b, t // tq),
        in_specs=[qspec(IDX_HEADS * IDX_DIM), qspec(IDX_HEADS), qspec(ATT_DIM),
                  kspec(LANES), kspec(ATT_DIM), kspec(ATT_DIM)],
        out_specs=qspec(ATT_DIM),
        scratch_shapes=[pltpu.VMEM((IDX_HEADS, tq, LANES), BF16),
                        pltpu.VMEM((N_HEADS, tq, LANES), BF16),
                        pltpu.VMEM((IDX_HEADS, tq, tk), F32),
                        pltpu.VMEM((t // tk, tq, tk), F32),
                        pltpu.VMEM((tq, 1), I32),
                        pltpu.VMEM((N_HEADS, tq, LANES), F32),
                        pltpu.VMEM((N_HEADS, tq, LANES), F32),
                        pltpu.VMEM((N_HEADS // 2, tq, LANES), F32)],
        compiler_params=_params(2),
        name="dsa_prompt",
    )(qi, wi, q, kid, kb, vb)


def _dsa_sample_kernel(pt_ref, qi_ref, wi_ref, q_ref, kidn_ref, kn_ref, vn_ref, *rest, n_pages, rows, past):
    idx_refs = rest[:n_pages]
    k_refs = rest[n_pages:2 * n_pages]
    v_refs = rest[2 * n_pages:3 * n_pages]
    o_ref, kid_all, k_all, v_all, tie_ref = rest[3 * n_pages:]
    n_keys = past + PAGE_SIZE

    for p in range(n_pages):
        sl = slice(p * PAGE_SIZE, (p + 1) * PAGE_SIZE)
        x = idx_refs[p][0]
        kid_all[sl, :] = jnp.concatenate([x, x], axis=1).astype(BF16)
        k_all[sl, :] = k_refs[p][0].astype(BF16)
        v_all[sl, :] = v_refs[p][0].astype(BF16)
    tail = slice(past, n_keys)
    pad = lambda x: jnp.concatenate([x, jnp.zeros((PAGE_SIZE - rows, x.shape[1]), F32)], axis=0).astype(BF16)
    kid_all[tail, :] = pad(kidn_ref[0])
    k_all[tail, :] = pad(kn_ref[0])
    v_all[tail, :] = pad(vn_ref[0])

    lo = lax.broadcasted_iota(I32, (rows, LANES), 1) < HEAD_DIM
    qi = qi_ref[0]
    parts = []
    for pr in range(IDX_HEADS // 2):
        parts.extend(_split_heads(qi[:, pr * LANES:(pr + 1) * LANES], lo))
    s_all = _dot_nt(jnp.concatenate(parts, axis=0).astype(BF16), kid_all[...])
    wi = wi_ref[0]
    acc = jnp.zeros((rows, n_keys), F32)
    for h in range(IDX_HEADS):
        acc = acc + wi[:, h:h + 1] * jnp.maximum(s_all[h * rows:(h + 1) * rows], 0.0)
    kpos = lax.broadcasted_iota(I32, (rows, n_keys), 1)
    qpos = lax.broadcasted_iota(I32, (rows, n_keys), 0) + past
    scores = jnp.where(kpos <= qpos, acc, NEG)

    def count(pred):
        return jnp.sum(pred(scores, kpos).astype(F32), axis=1, keepdims=True)

    n_bits = (n_keys - 1).bit_length()
    thr, need, n_eq = _select_threshold(count, rows)
    tie_ref[...] = jnp.full((rows, 1), INT_MAX, I32)
    split = n_eq > need

    @pl.when(jnp.max(jnp.where(split, 1.0, 0.0)) > 0.5)
    def _():
        tie_ref[...] = jnp.where(split, _tie_position(count, thr, need, n_bits), INT_MAX)

    bias = _bias_from_scores(scores, kpos, qpos, thr, tie_ref[...])
    bias2 = jnp.concatenate([bias, bias], axis=0)

    q = q_ref[0]
    for pr in range(N_HEADS // 2):
        sl = slice(pr * LANES, (pr + 1) * LANES)
        lhs = jnp.concatenate(_split_heads(q[:, sl], lo), axis=0).astype(BF16)
        s = _dot_nt(lhs, k_all[:, sl]) + bias2
        p = jnp.exp(s - jnp.max(s, axis=1, keepdims=True))
        l = jnp.sum(p, axis=1, keepdims=True)
        pv = _dot(p.astype(BF16), v_all[:, sl]) / l
        o_ref[0, :, sl] = jnp.where(lo, pv[:rows], pv[rows:])


def _dsa_sample(page_table, qi, wi, q, kidn, kn, vn, cache_idx, cache_k, cache_v):
    db, rows, _ = q.shape
    n_pages = page_table.shape[1]
    past = n_pages * PAGE_SIZE
    n_keys = past + PAGE_SIZE
    seq = lambda w: pl.BlockSpec((1, rows, w), lambda s, pt: (s, 0, 0))

    def page(w, p):
        return pl.BlockSpec((1, PAGE_SIZE, w), lambda s, pt: (pt[s * n_pages + p], 0, 0))

    grid_spec = pltpu.PrefetchScalarGridSpec(
        num_scalar_prefetch=1,
        grid=(db,),
        in_specs=[seq(IDX_HEADS * IDX_DIM), seq(IDX_HEADS), seq(ATT_DIM), seq(LANES), seq(ATT_DIM), seq(ATT_DIM)]
                 + [page(IDX_DIM, p) for p in range(n_pages)]
                 + [page(ATT_DIM, p) for p in range(n_pages)]
                 + [page(ATT_DIM, p) for p in range(n_pages)],
        out_specs=seq(ATT_DIM),
        scratch_shapes=[pltpu.VMEM((n_keys, LANES), BF16), pltpu.VMEM((n_keys, ATT_DIM), BF16),
                        pltpu.VMEM((n_keys, ATT_DIM), BF16), pltpu.VMEM((rows, 1), I32)],
    )
    return pl.pallas_call(
        functools.partial(_dsa_sample_kernel, n_pages=n_pages, rows=rows, past=past),
        out_shape=jax.ShapeDtypeStruct((db, rows, ATT_DIM), F32),
        grid_spec=grid_spec,
        compiler_params=_params(1),
        name="dsa_sample",
    )(page_table.reshape(-1), qi, wi, q, kidn, kn, vn,
      *([cache_idx] * n_pages), *([cache_k] * n_pages), *([cache_v] * n_pages))


def _out_kernel(h_ref, att_ref, conv_ref, sgc_ref, sga_ref, g2_ref, sh_ref, sc_ref, g3_ref, gn_ref, gfin_ref,
                wao_ref, wo_ref, wg_ref, wu_ref, wd_ref, y_ref, hn_ref, acc_ref):
    att_o = _dot(att_ref[0], wao_ref[...])
    mix = sgc_ref[0].astype(F32) * conv_ref[0].astype(F32) + sga_ref[0].astype(F32) * att_o
    h2 = h_ref[0] + g2_ref[0] * _dot(mix.astype(BF16), wo_ref[...])
    y_ref[0] = h2
    hn_ref[...] = _rms_mod(h2, gn_ref[...], sc_ref[0], sh_ref[0]).astype(BF16)
    _swiglu_into(hn_ref, wg_ref, wu_ref, wd_ref, acc_ref)
    h3 = y_ref[0] + 0.5 * g3_ref[0] * acc_ref[...]
    ms = jnp.mean(h3 * h3, axis=-1, keepdims=True)
    y_ref[0] = h3 * lax.rsqrt(ms + EPS) * gfin_ref[...]


def _out(h, att, conv, sgc, sga, g2, shift, scale, g3, gn, gfin, wao, wo, wg, wu, wd):
    s, t, _ = h.shape
    tm = min(TOKEN_TILE, t)
    per_token = shift.shape[1] != 1
    mod = _tile_spec(tm, D_MODEL, per_token)
    tok = lambda w: _tile_spec(tm, w, True)
    return pl.pallas_call(
        _out_kernel,
        out_shape=jax.ShapeDtypeStruct(h.shape, F32),
        grid=(s, t // tm),
        in_specs=[tok(D_MODEL), tok(ATT_DIM), tok(D_MODEL), tok(D_MODEL), tok(D_MODEL), mod, mod, mod, mod,
                  _const_spec((1, D_MODEL)), _const_spec((1, D_MODEL)),
                  _const_spec(wao.shape), _const_spec(wo.shape),
                  _const_spec(wg.shape), _const_spec(wu.shape), _const_spec(wd.shape)],
        out_specs=tok(D_MODEL),
        scratch_shapes=[pltpu.VMEM((tm, D_MODEL), BF16), pltpu.VMEM((tm, D_MODEL), F32)],
        compiler_params=_params(2),
        name="out_ffn2",
    )(h, att, conv, sgc, sga, g2, shift, scale, g3, gn, gfin, wao, wo, wg, wu, wd)


def _rope_tables(pos):
    inv = ROPE_THETA ** (-jnp.arange(0, HEAD_DIM, 2, dtype=F32) / HEAD_DIM)
    ang = pos.astype(F32)[:, None] * inv[None, :]
    cos, sin = jnp.cos(ang), jnp.sin(ang)
    return jnp.concatenate([cos] * 4, axis=1), jnp.concatenate([-sin, sin, -sin, sin], axis=1)


def _ff_weights(wg, wu, wd):
    split_cols = lambda w: w.reshape(D_MODEL, N_FF_CHUNKS, FF_CHUNK).transpose(1, 0, 2).astype(BF16)
    return split_cols(wg), split_cols(wu), wd.reshape(N_FF_CHUNKS, FF_CHUNK, D_MODEL).astype(BF16)


def _in_weights(w_in):
    cuts = [0]
    for width in (D_CONV, D_CONV, ATT_DIM, ATT_DIM, ATT_DIM, IDX_HEADS * IDX_DIM, IDX_DIM, IDX_HEADS,
                  D_MODEL, D_MODEL):
        cuts.append(cuts[-1] + width)
    piece = lambda a, b: w_in[:, cuts[a]:cuts[b]]
    w_ki, w_wi = piece(6, 7), piece(7, 8)
    w_kw = jnp.concatenate([w_ki, w_ki, w_wi, jnp.zeros((D_MODEL, 2 * LANES - 2 * IDX_DIM - IDX_HEADS), F32)],
                           axis=1)
    return [w.astype(BF16) for w in (piece(0, 2), piece(2, 4), piece(4, 5), piece(5, 6), w_kw, piece(8, 10))]


def kernel(x_prompt, x_sample, cache_k, cache_v, cache_idx_k, state_conv, page_table, c_prompt, c_sample,
           w_ada, b_ada, g_ffn1, w1_gate, w1_up, w1_down, g_mix, w_in, w_dw, b_dw, ln_g, ln_b,
           w_conv_out, w_attn_o, w_out, g_ffn2, w2_gate, w2_up, w2_down, g_final):
    bsz, t, _ = x_prompt.shape
    db, s, _ = x_sample.shape
    depth = w_ada.shape[0]
    n_pool = cache_k.shape[1]
    past = page_table.shape[1] * PAGE_SIZE
    q_rows = 8
    assert depth == 1 and t >= CONV_W - 1 and s <= q_rows

    cos_p, sin_p = _rope_tables(jnp.arange(t))
    cos_s, sin_s = _rope_tables(past + jnp.arange(s))
    cos_s, sin_s = jnp.tile(cos_s, (db, 1)), jnp.tile(sin_s, (db, 1))
    row = lambda v: v.reshape(1, -1)

    c_all = jnp.concatenate([c_prompt, c_sample], axis=0)
    n_c = c_all.shape[0]
    c_all = jnp.pad(c_all, ((0, -n_c % 8), (0, 0)))

    hp = x_prompt
    hs = x_sample.reshape(1, db * s, D_MODEL)
    outs = [[] for _ in range(8)]
    for l in range(depth):
        m = _ada(c_all, w_ada[l], b_ada[l])
        mod_p = [m[:bsz, i * D_MODEL:(i + 1) * D_MODEL].reshape(bsz, 1, D_MODEL) for i in range(N_MOD)]
        mod_s = [jnp.repeat(m[bsz:n_c, i * D_MODEL:(i + 1) * D_MODEL], s, axis=0).reshape(1, db * s, D_MODEL)
                 for i in range(N_MOD)]
        ff1 = _ff_weights(w1_gate[l], w1_up[l], w1_down[l])
        ff2 = _ff_weights(w2_gate[l], w2_up[l], w2_down[l])
        w_ins = _in_weights(w_in[l])
        conv_w = (w_dw[l], row(b_dw[l]), row(ln_g[l]), row(ln_b[l]), w_conv_out[l].astype(BF16))
        wao, wo = w_attn_o[l].astype(BF16), w_out[l].astype(BF16)

        def front(h, mod, cos, sin):
            h = _ffn(h, mod[0], mod[1], mod[2], row(g_ffn1[l]), *ff1)
            return h, _inproj(h, mod[3], mod[4], row(g_mix[l]), cos, sin, w_ins)

        def back(h, att, conv, sgc, sga, mod):
            return _out(h, att, conv, sgc, sga, mod[5], mod[6], mod[7], mod[8], row(g_ffn2[l]), row(g_final),
                        wao, wo, *ff2)

        hp, (u, q, k, kb, v, vb, qi, kid, ki, wi, sgc, sga) = front(hp, mod_p, cos_p, sin_p)
        conv = _conv_prompt(u, *conv_w)
        att = _dsa_prompt(qi, wi, q, kid, kb, vb)
        hp = back(hp, att, conv, sgc, sga, mod_p)
        outs[0].append(k.reshape(bsz, t, N_HEADS, HEAD_DIM))
        outs[1].append(v.reshape(bsz, t, N_HEADS, HEAD_DIM))
        outs[2].append(ki)
        outs[3].append(u[:, t - (CONV_W - 1):])

        hs, (u, q, k, kb, v, vb, qi, kid, ki, wi, sgc, sga) = front(hs, mod_s, cos_s, sin_s)
        per_seq = lambda a: jnp.pad(a.astype(F32).reshape(db, s, -1), ((0, 0), (0, q_rows - s), (0, 0)))
        ext = jnp.concatenate([state_conv[l], u.reshape(db, s, D_CONV)], axis=1)
        conv = _conv_sample(ext.transpose(1, 0, 2), *conv_w).transpose(1, 0, 2).reshape(1, db * s, D_MODEL)
        att = _dsa_sample(page_table, per_seq(qi), per_seq(wi), per_seq(q), per_seq(kid), per_seq(k), per_seq(v),
                          cache_idx_k[l], cache_k[l].reshape(n_pool, PAGE_SIZE, ATT_DIM),
                          cache_v[l].reshape(n_pool, PAGE_SIZE, ATT_DIM))
        att = att[:, :s].reshape(1, db * s, ATT_DIM).astype(BF16)
        hs = back(hs, att, conv, sgc, sga, mod_s)
        outs[4].append(k.reshape(db, s, N_HEADS, HEAD_DIM))
        outs[5].append(v.reshape(db, s, N_HEADS, HEAD_DIM))
        outs[6].append(ki.reshape(db, s, IDX_DIM))
        outs[7].append(ext[:, s:])

    return (hp, hs.reshape(db, s, D_MODEL)) + tuple(jnp.stack(o) for o in outs)
```

```python
import functools

import jax
import jax.numpy as jnp
from jax import lax
from jax.experimental import pallas as pl
from jax.experimental.pallas import tpu as pltpu

F32 = jnp.float32
BF16 = jnp.bfloat16
I32 = jnp.int32

D_MODEL = 1024
N_HEADS = 8
HEAD_DIM = 64
ATT_DIM = N_HEADS * HEAD_DIM
IDX_HEADS = 8
IDX_DIM = 64
D_CONV = D_MODEL // 2
CONV_W = 31
D_FF = ((8 * D_MODEL // 3 + 255) // 256) * 256
N_MOD = 9
TOPK_MAX = 256
PAGE_SIZE = 128
ROPE_THETA = 10000.0
EPS = 1e-6
NEG = -1e30

LANES = 128
FF_CHUNK = 256
N_FF_CHUNKS = D_FF // FF_CHUNK
TOKEN_TILE = 512
CONV_HALO = 32
CONV_ROWS = 64
DSA_TQ = 256
DSA_TK = 256
DSA_STRIP = 64
VMEM_LIMIT = 56 * 1024 * 1024
INT_MIN = -2 ** 31
INT_MAX = 2 ** 31 - 1
KEY_NEG_INF = INT_MIN + 2 ** 23 - 1


def _dot(a, b):
    return jnp.dot(a, b, preferred_element_type=F32)


def _dot_nt(a, b):
    return lax.dot_general(a, b, (((1,), (1,)), ((), ())), preferred_element_type=F32)


def _sigmoid(x):
    return jax.nn.sigmoid(x)


def _rms_mod(x, g, scale, shift):
    ms = jnp.mean(x * x, axis=-1, keepdims=True)
    return (x * lax.rsqrt(ms + EPS)) * g * (1.0 + scale) + shift


def _key_to_f32(key):
    return pltpu.bitcast(key ^ ((key >> 31) & INT_MAX), F32)


def _const_spec(shape):
    nd = len(shape)
    return pl.BlockSpec(shape, lambda *_: (0,) * nd, pipeline_mode=pl.Buffered(1))


def _params(n_grid):
    return pltpu.CompilerParams(dimension_semantics=("arbitrary",) * n_grid, vmem_limit_bytes=VMEM_LIMIT)


def _ada_kernel(c_ref, w_ref, b_ref, o_ref):
    c = c_ref[...]
    s = (c * _sigmoid(c)).astype(BF16)
    o_ref[...] = _dot(s, w_ref[...].astype(BF16)) + b_ref[...]


def _ada(c, w, b):
    rows = c.shape[0]
    nb = D_MODEL
    return pl.pallas_call(
        _ada_kernel,
        out_shape=jax.ShapeDtypeStruct((rows, N_MOD * D_MODEL), F32),
        grid=(N_MOD * D_MODEL // nb,),
        in_specs=[pl.BlockSpec((rows, D_MODEL), lambda j: (0, 0)),
                  pl.BlockSpec((D_MODEL, nb), lambda j: (0, j)),
                  pl.BlockSpec((1, nb), lambda j: (0, j))],
        out_specs=pl.BlockSpec((rows, nb), lambda j: (0, j)),
        compiler_params=_params(1),
        name="ada",
    )(c, w, b.reshape(1, -1))


def _swiglu_into(hn_ref, wg_ref, wu_ref, wd_ref, acc_ref):
    acc_ref[...] = jnp.zeros_like(acc_ref)

    def body(i, carry):
        hn = hn_ref[...]
        a = _dot(hn, wg_ref[i])
        b = _dot(hn, wu_ref[i])
        act = (a * _sigmoid(a) * b).astype(BF16)
        acc_ref[...] += _dot(act, wd_ref[i])
        return carry

    lax.fori_loop(0, N_FF_CHUNKS, body, 0)


def _ffn_kernel(x_ref, sh_ref, sc_ref, gt_ref, g_ref, wg_ref, wu_ref, wd_ref, o_ref, hn_ref, acc_ref):
    x = x_ref[0]
    hn_ref[...] = _rms_mod(x, g_ref[...], sc_ref[0], sh_ref[0]).astype(BF16)
    _swiglu_into(hn_ref, wg_ref, wu_ref, wd_ref, acc_ref)
    o_ref[0] = x + 0.5 * gt_ref[0] * acc_ref[...]


def _tile_spec(tm, width, per_token):
    if per_token:
        return pl.BlockSpec((1, tm, width), lambda b, i: (b, i, 0))
    return pl.BlockSpec((1, 1, width), lambda b, i: (b, 0, 0))


def _ffn(x, shift, scale, gate, g, wg, wu, wd):
    s, t, _ = x.shape
    tm = min(TOKEN_TILE, t)
    per_token = shift.shape[1] != 1
    tok = _tile_spec(tm, D_MODEL, True)
    mod = _tile_spec(tm, D_MODEL, per_token)
    return pl.pallas_call(
        _ffn_kernel,
        out_shape=jax.ShapeDtypeStruct(x.shape, F32),
        grid=(s, t // tm),
        in_specs=[tok, mod, mod, mod, _const_spec((1, D_MODEL)),
                  _const_spec(wg.shape), _const_spec(wu.shape), _const_spec(wd.shape)],
        out_specs=tok,
        scratch_shapes=[pltpu.VMEM((tm, D_MODEL), BF16), pltpu.VMEM((tm, D_MODEL), F32)],
        compiler_params=_params(2),
        name="ffn1",
    )(x, shift, scale, gate, g, wg, wu, wd)


def _rope(x, cos, sin, lo32):
    sw = jnp.where(lo32, pltpu.roll(x, 96, 1), pltpu.roll(x, 32, 1))
    return x * cos + sw * sin


def _rope_t(x, cos, sin):
    h = HEAD_DIM // 2
    sw = jnp.concatenate([x[h:2 * h], x[0:h], x[3 * h:4 * h], x[2 * h:3 * h]], axis=0)
    return x * cos + sw * sin


def _inproj_kernel(h_ref, sh_ref, sc_ref, g_ref, cos_ref, sin_ref, cost_ref, sint_ref,
                   wglu_ref, wq_ref, wqw_ref, wgt_ref, wkvt_ref,
                   u_ref, q_ref, qi_ref, wi_ref, sgc_ref, sga_ref,
                   kt_ref, kbt_ref, vt_ref, vbt_ref, kidt_ref, kit_ref, n_ref, *, tm, tk):
    n_ref[...] = _rms_mod(h_ref[0], g_ref[...], sc_ref[0], sh_ref[0]).astype(BF16)
    lane = lax.broadcasted_iota(I32, (1, LANES), 1)
    lo32 = (lane & (HEAD_DIM - 1)) < HEAD_DIM // 2
    rope = functools.partial(_rope, cos=cos_ref[...], sin=sin_ref[...], lo32=lo32)
    n_q = ATT_DIM // LANES

    z = _dot(n_ref[...], wglu_ref[...])
    u_ref[0] = z[:, :D_CONV] * _sigmoid(z[:, D_CONV:])

    z = _dot(n_ref[...], wq_ref[...])
    for s in range(n_q):
        sl = slice(s * LANES, (s + 1) * LANES)
        q_ref[0, :, sl] = (rope(z[:, sl]) * HEAD_DIM ** -0.5).astype(BF16)

    z = _dot(n_ref[...], wqw_ref[...])
    for s in range(n_q):
        sl = slice(s * LANES, (s + 1) * LANES)
        qi_ref[0, :, sl] = rope(z[:, sl]).astype(BF16)
    wi_ref[0] = z[:, ATT_DIM:ATT_DIM + IDX_HEADS] * (IDX_HEADS * IDX_DIM) ** -0.5

    z = _dot(n_ref[...], wgt_ref[...])
    sgc_ref[0] = _sigmoid(z[:, :D_MODEL]).astype(BF16)
    sga_ref[0] = _sigmoid(z[:, D_MODEL:]).astype(BF16)

    zt = _dot_nt(wkvt_ref[...], n_ref[...])
    cost, sint = cost_ref[...], sint_ref[...]
    chunks = [slice(j * tk, (j + 1) * tk) for j in range(tm // tk)]
    for s in range(n_q):
        rows = slice(s * LANES, (s + 1) * LANES)
        xk = _rope_t(zt[rows], cost, sint)
        kt_ref[0, rows, :] = xk
        for j, ch in enumerate(chunks):
            kbt_ref[0, j, rows, :] = xk[:, ch].astype(BF16)
    zv = zt[ATT_DIM:2 * ATT_DIM]
    vt_ref[0] = zv
    for j, ch in enumerate(chunks):
        vbt_ref[0, j] = zv[:, ch].astype(BF16)
    kid = _rope_t(zt[2 * ATT_DIM:], cost, sint)
    kit_ref[0] = kid[:IDX_DIM]
    for j, ch in enumerate(chunks):
        kidt_ref[0, j] = kid[:, ch].astype(BF16)


def _inproj(h, shift, scale, g, cos, sin, cost, sint, ws):
    s, t, _ = h.shape
    tm = min(TOKEN_TILE, t)
    tk = DSA_TK
    per_token = shift.shape[1] != 1
    mod = _tile_spec(tm, D_MODEL, per_token)
    tab = pl.BlockSpec((tm, LANES), lambda b, i: (i, 0))
    tabt = pl.BlockSpec((LANES, tm), lambda b, i: (0, i))
    tok = [(D_CONV, F32), (ATT_DIM, BF16), (IDX_HEADS * IDX_DIM, BF16), (IDX_HEADS, F32),
           (D_MODEL, BF16), (D_MODEL, BF16)]
    feat = lambda rows, dt: (jax.ShapeDtypeStruct((s, rows, t), dt),
                             pl.BlockSpec((1, rows, tm), lambda b, i: (b, 0, i)))
    chunked = lambda rows: (jax.ShapeDtypeStruct((s, t // tk, rows, tk), BF16),
                            pl.BlockSpec((1, tm // tk, rows, tk), lambda b, i: (b, i, 0, 0)))
    key_side = [feat(ATT_DIM, F32), chunked(ATT_DIM), feat(ATT_DIM, F32), chunked(ATT_DIM),
                chunked(LANES), feat(IDX_DIM, F32)]
    return pl.pallas_call(
        functools.partial(_inproj_kernel, tm=tm, tk=tk),
        out_shape=[jax.ShapeDtypeStruct((s, t, w), dt) for w, dt in tok] + [sd for sd, _ in key_side],
        grid=(s, t // tm),
        in_specs=[_tile_spec(tm, D_MODEL, True), mod, mod, _const_spec((1, D_MODEL)), tab, tab, tabt, tabt]
                 + [_const_spec(w.shape) for w in ws],
        out_specs=[_tile_spec(tm, w, True) for w, _ in tok] + [sp for _, sp in key_side],
        scratch_shapes=[pltpu.VMEM((tm, D_MODEL), BF16)],
        compiler_params=_params(2),
        name="inproj",
    )(h, shift, scale, g, cos, sin, cost, sint, *ws)


def _ln_silu(y, g, b):
    mu = jnp.mean(y, axis=-1, keepdims=True)
    yc = y - mu
    var = jnp.mean(yc * yc, axis=-1, keepdims=True)
    y = yc * lax.rsqrt(var + EPS) * g + b
    return y * _sigmoid(y)


def _conv_kernel(u_ref, up_ref, wdw_ref, bdw_ref, lng_ref, lnb_ref, wco_ref, o_ref, ext_ref, y_ref, *, tm):
    i = pl.program_id(1)
    ext_ref[0:CONV_HALO, :] = jnp.where(i > 0, up_ref[0], 0.0)
    ext_ref[CONV_HALO:, :] = u_ref[0]
    off = CONV_HALO - (CONV_W - 1)
    for r in range(tm // CONV_ROWS):
        acc = jnp.broadcast_to(bdw_ref[...], (CONV_ROWS, D_CONV))
        for j in range(CONV_W):
            lo = r * CONV_ROWS + off + j
            acc = acc + wdw_ref[j:j + 1, :] * ext_ref[lo:lo + CONV_ROWS, :]
        y_ref[r * CONV_ROWS:(r + 1) * CONV_ROWS, :] = _ln_silu(acc, lng_ref[...], lnb_ref[...]).astype(BF16)
    o_ref[0] = _dot(y_ref[...], wco_ref[...]).astype(BF16)


def _conv_prompt(u, wdw, bdw, lng, lnb, wco):
    b, t, _ = u.shape
    tm = min(TOKEN_TILE, t)
    halo_blocks = tm // CONV_HALO
    return pl.pallas_call(
        functools.partial(_conv_kernel, tm=tm),
        out_shape=jax.ShapeDtypeStruct((b, t, D_MODEL), BF16),
        grid=(b, t // tm),
        in_specs=[pl.BlockSpec((1, tm, D_CONV), lambda s, i: (s, i, 0)),
                  pl.BlockSpec((1, CONV_HALO, D_CONV), lambda s, i: (s, jnp.maximum(i * halo_blocks - 1, 0), 0)),
                  _const_spec(wdw.shape), _const_spec(bdw.shape), _const_spec(lng.shape), _const_spec(lnb.shape),
                  _const_spec(wco.shape)],
        out_specs=pl.BlockSpec((1, tm, D_MODEL), lambda s, i: (s, i, 0)),
        scratch_shapes=[pltpu.VMEM((tm + CONV_HALO, D_CONV), F32), pltpu.VMEM((tm, D_CONV), BF16)],
        compiler_params=_params(2),
        name="conv_prompt",
    )(u, u, wdw, bdw, lng, lnb, wco)


def _conv_sample_kernel(ext_ref, wdw_ref, bdw_ref, lng_ref, lnb_ref, wco_ref, o_ref, *, steps, rows):
    for s in range(steps):
        acc = jnp.broadcast_to(bdw_ref[...], (rows, D_CONV))
        for j in range(CONV_W):
            acc = acc + wdw_ref[j:j + 1, :] * ext_ref[s + j]
        y = _ln_silu(acc, lng_ref[...], lnb_ref[...]).astype(BF16)
        o_ref[s] = _dot(y, wco_ref[...]).astype(BF16)


def _conv_sample(ext_t, wdw, bdw, lng, lnb, wco):
    n, rows, _ = ext_t.shape
    steps = n - (CONV_W - 1)
    return pl.pallas_call(
        functools.partial(_conv_sample_kernel, steps=steps, rows=rows),
        out_shape=jax.ShapeDtypeStruct((steps, rows, D_MODEL), BF16),
        grid=(1,),
        in_specs=[_const_spec(ext_t.shape), _const_spec(wdw.shape), _const_spec(bdw.shape),
                  _const_spec(lng.shape), _const_spec(lnb.shape), _const_spec(wco.shape)],
        out_specs=pl.BlockSpec((steps, rows, D_MODEL), lambda i: (0, 0, 0)),
        compiler_params=_params(1),
        name="conv_sample",
    )(ext_t, wdw, bdw, lng, lnb, wco)


def _count_term(kind, scores, kpos, thr, pos):
    if kind == "ge":
        return jnp.where(scores >= thr, 1, 0)
    if kind == "gt":
        return jnp.where(scores > thr, 1, 0)
    if kind == "eq":
        return jnp.where(scores == thr, 1, 0)
    return jnp.where(scores == thr, jnp.where(kpos < pos, 1, 0), 0)


def _select_threshold(count, rows):
    def bit_body(b, key):
        cand = key + lax.shift_left(jnp.int32(1), 31 - b)
        return jnp.where(count("ge", _key_to_f32(cand), None) >= TOPK_MAX, cand, key)

    key = lax.fori_loop(0, 32, bit_body, jnp.full((rows, 1), INT_MIN, I32))
    thr = jnp.where(key < KEY_NEG_INF, -jnp.inf, _key_to_f32(key))
    need = TOPK_MAX - count("gt", thr, None)
    n_eq = count("eq", thr, None)
    return thr, need, n_eq


def _tie_position(count, thr, need, n_bits):
    def bit_body(b, j):
        cand = j + lax.shift_left(jnp.int32(1), n_bits - 1 - b)
        return jnp.where(count("tie", thr, cand) < need, cand, j)

    return lax.fori_loop(0, n_bits, bit_body, jnp.zeros(thr.shape, I32))


def _bias_from_scores(scores, kpos, qpos, thr, last_tie):
    t1 = jnp.where(kpos <= last_tie, 0.0, NEG)
    t2 = jnp.where(scores == thr, t1, NEG)
    t3 = jnp.where(scores > thr, 0.0, t2)
    return jnp.where(kpos <= qpos, t3, NEG)


def _split_heads(x, lo):
    zero = jnp.zeros_like(x)
    return jnp.where(lo, x, zero), jnp.where(lo, zero, x)


def _dsa_prompt_kernel(qi_ref, wi_ref, q_ref, kid_ref, kb_ref, vb_ref, o_ref,
                       qim_ref, qm_ref, wib_ref, sc_ref, s8_ref, s_ref, p_ref, thr_ref, pos_ref, tie_ref,
                       m_ref, l_ref, al_ref, acc_ref, *, tq, tk, n_bits):
    st = DSA_STRIP
    n_st = tq // st
    n_lt = tk // LANES
    q0 = pl.program_id(1) * tq
    n_ch = pl.program_id(1) + 1
    lo = lax.broadcasted_iota(I32, (tq, LANES), 1) < HEAD_DIM
    strips = [slice(r * st, (r + 1) * st) for r in range(n_st)]
    tiles = [slice(t * LANES, (t + 1) * LANES) for t in range(n_lt)]
    lane_s = lax.broadcasted_iota(I32, (st, LANES), 1)
    row_s = lax.broadcasted_iota(I32, (st, LANES), 0)

    for pr in range(N_HEADS // 2):
        sl = slice(pr * LANES, (pr + 1) * LANES)
        qim_ref[2 * pr], qim_ref[2 * pr + 1] = _split_heads(qi_ref[0, :, sl], lo)
        qm_ref[pr, 0:tq], qm_ref[pr, tq:2 * tq] = _split_heads(q_ref[0, :, sl], lo)
    for h in range(IDX_HEADS):
        wib_ref[h] = jnp.broadcast_to(wi_ref[0, :, h:h + 1], (tq, tk))

    def score_chunk(c, carry):
        kc = kid_ref[0, c]
        for h in range(IDX_HEADS):
            s8_ref[h] = _dot(qim_ref[h], kc)
        for r, rows in enumerate(strips):
            acc = jnp.zeros((st, tk), F32)
            for h in range(IDX_HEADS):
                acc = acc + wib_ref[h, rows, :] * jnp.maximum(s8_ref[h, rows, :], 0.0)
            for t, tl in enumerate(tiles):
                visible = lane_s + (c * tk + t * LANES) <= row_s + (q0 + r * st)
                sc_ref[c, rows, tl] = jnp.where(visible, acc[:, tl], NEG)
        return carry

    lax.fori_loop(0, n_ch, score_chunk, 0)

    def count(kind, thr, pos):
        thr_ref[...] = jnp.broadcast_to(thr, (tq, LANES))
        if pos is not None:
            pos_ref[...] = jnp.broadcast_to(pos, (tq, LANES))

        def body(c, accs):
            out = []
            for r, rows in enumerate(strips):
                thr_b = thr_ref[rows, :]
                pos_b = pos_ref[rows, :] if pos is not None else None
                a = accs[r]
                for t, tl in enumerate(tiles):
                    kpos = lane_s + (c * tk + t * LANES)
                    a = a + _count_term(kind, sc_ref[c, rows, tl], kpos, thr_b, pos_b)
                out.append(a)
            return tuple(out)

        accs = lax.fori_loop(0, n_ch, body, tuple(jnp.zeros((st, LANES), I32) for _ in strips))
        return jnp.concatenate([jnp.sum(a.astype(F32), axis=1, keepdims=True) for a in accs], axis=0)

    thr, need, n_eq = _select_threshold(count, tq)
    tie_ref[...] = jnp.full((tq, LANES), INT_MAX, I32)
    split = n_eq > need

    @pl.when(jnp.max(jnp.where(split, 1.0, 0.0)) > 0.5)
    def _():
        j = _tie_position(count, thr, need, n_bits)
        tie_ref[...] = jnp.broadcast_to(jnp.where(split, j, INT_MAX), (tq, LANES))

    thr_ref[...] = jnp.broadcast_to(thr, (tq, LANES))

    def to_bias(c, carry):
        for r, rows in enumerate(strips):
            thr_b, tie_b = thr_ref[rows, :], tie_ref[rows, :]
            qpos = row_s + (q0 + r * st)
            for t, tl in enumerate(tiles):
                kpos = lane_s + (c * tk + t * LANES)
                sc_ref[c, rows, tl] = _bias_from_scores(sc_ref[c, rows, tl], kpos, qpos, thr_b, tie_b)
        return carry

    lax.fori_loop(0, n_ch, to_bias, 0)

    m_ref[...] = jnp.full(m_ref.shape, NEG, F32)
    l_ref[...] = jnp.zeros(l_ref.shape, F32)
    acc_ref[...] = jnp.zeros(acc_ref.shape, F32)

    def attend_chunk(c, carry):
        for pr in range(N_HEADS // 2):
            feat = slice(pr * LANES, (pr + 1) * LANES)
            s_ref[...] = _dot(qm_ref[pr], kb_ref[0, c, feat, :])
            for e in range(2):
                h = 2 * pr + e
                for r, rows in enumerate(strips):
                    srows = slice(e * tq + r * st, e * tq + (r + 1) * st)
                    s = [s_ref[srows, tl] + sc_ref[c, rows, tl] for tl in tiles]
                    mx = s[0]
                    for x in s[1:]:
                        mx = jnp.maximum(mx, x)
                    m_prev = m_ref[h, rows, :]
                    m_new = jnp.maximum(m_prev, jnp.max(mx, axis=1, keepdims=True))
                    alpha = jnp.exp(m_prev - m_new)
                    p = [jnp.exp(x - m_new) for x in s]
                    psum = p[0]
                    for x in p[1:]:
                        psum = psum + x
                    l_ref[h, rows, :] = alpha * l_ref[h, rows, :] + psum
                    m_ref[h, rows, :] = m_new
                    al_ref[e, rows, :] = alpha
                    for x, tl in zip(p, tiles):
                        p_ref[srows, tl] = x.astype(BF16)
            pv = _dot_nt(p_ref[...], vb_ref[0, c, feat, :])
            a = acc_ref[pr]
            acc_ref[pr] = jnp.where(lo, a * al_ref[0] + pv[:tq], a * al_ref[1] + pv[tq:])
        return carry

    lax.fori_loop(0, n_ch, attend_chunk, 0)

    for pr in range(N_HEADS // 2):
        l_e = jnp.sum(l_ref[2 * pr], axis=1, keepdims=True)
        l_o = jnp.sum(l_ref[2 * pr + 1], axis=1, keepdims=True)
        o_ref[0, :, pr * LANES:(pr + 1) * LANES] = (acc_ref[pr] * jnp.where(lo, 1.0 / l_e, 1.0 / l_o)).astype(BF16)


def _dsa_prompt(qi, wi, q, kidt, kbt, vbt):
    b, t, _ = q.shape
    tq, tk = DSA_TQ, DSA_TK
    assert tq == tk and t % tq == 0
    n_bits = max(1, (t - 1).bit_length())
    qspec = lambda w: pl.BlockSpec((1, tq, w), lambda s, i: (s, i, 0))
    kspec = lambda rows: pl.BlockSpec((1, t // tk, rows, tk), lambda s, i: (s, 0, 0, 0),
                                      pipeline_mode=pl.Buffered(1))
    return pl.pallas_call(
        functools.partial(_dsa_prompt_kernel, tq=tq, tk=tk, n_bits=n_bits),
        out_shape=jax.ShapeDtypeStruct((b, t, ATT_DIM), BF16),
        grid=(b, t // tq),
        in_specs=[qspec(IDX_HEADS * IDX_DIM), qspec(IDX_HEADS), qspec(ATT_DIM),
                  kspec(LANES), kspec(ATT_DIM), kspec(ATT_DIM)],
        out_specs=qspec(ATT_DIM),
        scratch_shapes=[pltpu.VMEM((IDX_HEADS, tq, LANES), BF16),
                        pltpu.VMEM((N_HEADS // 2, 2 * tq, LANES), BF16),
                        pltpu.VMEM((IDX_HEADS, tq, tk), F32),
                        pltpu.VMEM((t // tk, tq, tk), F32),
                        pltpu.VMEM((IDX_HEADS, tq, tk), F32),
                        pltpu.VMEM((2 * tq, tk), F32),
                        pltpu.VMEM((2 * tq, tk), BF16),
                        pltpu.VMEM((tq, LANES), F32),
                        pltpu.VMEM((tq, LANES), I32),
                        pltpu.VMEM((tq, LANES), I32),
                        pltpu.VMEM((N_HEADS, tq, LANES), F32),
                        pltpu.VMEM((N_HEADS, tq, LANES), F32),
                        pltpu.VMEM((2, tq, LANES), F32),
                        pltpu.VMEM((N_HEADS // 2, tq, LANES), F32)],
        compiler_params=_params(2),
        name="dsa_prompt",
    )(qi, wi, q, kidt, kbt, vbt)


def _dsa_sample_kernel(pt_ref, qi_ref, wi_ref, q_ref, kidn_ref, kn_ref, vn_ref, *rest, n_pages, rows, past):
    idx_refs = rest[:n_pages]
    k_refs = rest[n_pages:2 * n_pages]
    v_refs = rest[2 * n_pages:3 * n_pages]
    o_ref, kid_all, k_all, v_all, tie_ref = rest[3 * n_pages:]
    n_keys = past + PAGE_SIZE

    for p in range(n_pages):
        sl = slice(p * PAGE_SIZE, (p + 1) * PAGE_SIZE)
        x = idx_refs[p][0]
        kid_all[:, sl] = jnp.concatenate([x, x], axis=0).astype(BF16)
        k_all[:, sl] = k_refs[p][0].astype(BF16)
        v_all[:, sl] = v_refs[p][0].astype(BF16)
    tail = slice(past, n_keys)
    kid_all[:, tail] = kidn_ref[0]
    k_all[:, tail] = kn_ref[0]
    v_all[:, tail] = vn_ref[0]

    lo = lax.broadcasted_iota(I32, (rows, LANES), 1) < HEAD_DIM
    qi = qi_ref[0]
    parts = []
    for pr in range(IDX_HEADS // 2):
        parts.extend(_split_heads(qi[:, pr * LANES:(pr + 1) * LANES], lo))
    s_all = _dot(jnp.concatenate(parts, axis=0).astype(BF16), kid_all[...])
    wi = wi_ref[0]
    acc = jnp.zeros((rows, n_keys), F32)
    for h in range(IDX_HEADS):
        acc = acc + wi[:, h:h + 1] * jnp.maximum(s_all[h * rows:(h + 1) * rows], 0.0)
    kpos = lax.broadcasted_iota(I32, (rows, n_keys), 1)
    qpos = lax.broadcasted_iota(I32, (rows, n_keys), 0) + past
    scores = jnp.where(kpos <= qpos, acc, NEG)

    def count(kind, thr, pos):
        return jnp.sum(_count_term(kind, scores, kpos, thr, pos).astype(F32), axis=1, keepdims=True)

    n_bits = (n_keys - 1).bit_length()
    thr, need, n_eq = _select_threshold(count, rows)
    tie_ref[...] = jnp.full((rows, 1), INT_MAX, I32)
    split = n_eq > need

    @pl.when(jnp.max(jnp.where(split, 1.0, 0.0)) > 0.5)
    def _():
        tie_ref[...] = jnp.where(split, _tie_position(count, thr, need, n_bits), INT_MAX)

    bias = _bias_from_scores(scores, kpos, qpos, thr, tie_ref[...])
    bias2 = jnp.concatenate([bias, bias], axis=0)

    q = q_ref[0]
    for pr in range(N_HEADS // 2):
        sl = slice(pr * LANES, (pr + 1) * LANES)
        lhs = jnp.concatenate(_split_heads(q[:, sl], lo), axis=0).astype(BF16)
        s = _dot(lhs, k_all[sl, :]) + bias2
        p = jnp.exp(s - jnp.max(s, axis=1, keepdims=True))
        l = jnp.sum(p, axis=1, keepdims=True)
        pv = _dot_nt(p.astype(BF16), v_all[sl, :]) / l
        o_ref[0, :, sl] = jnp.where(lo, pv[:rows], pv[rows:])


def _dsa_sample(page_table, qi, wi, q, kidn, kn, vn, cache_idx, cache_k, cache_v):
    db, rows, _ = q.shape
    n_pages = page_table.shape[1]
    past = n_pages * PAGE_SIZE
    n_keys = past + PAGE_SIZE
    seq = lambda w: pl.BlockSpec((1, rows, w), lambda s, pt: (s, 0, 0))
    new = lambda f: pl.BlockSpec((1, f, PAGE_SIZE), lambda s, pt: (s, 0, 0))

    def page(f, p):
        return pl.BlockSpec((1, f, PAGE_SIZE), lambda s, pt: (pt[s * n_pages + p], 0, 0))

    grid_spec = pltpu.PrefetchScalarGridSpec(
        num_scalar_prefetch=1,
        grid=(db,),
        in_specs=[seq(IDX_HEADS * IDX_DIM), seq(IDX_HEADS), seq(ATT_DIM), new(LANES), new(ATT_DIM), new(ATT_DIM)]
                 + [page(IDX_DIM, p) for p in range(n_pages)]
                 + [page(ATT_DIM, p) for p in range(n_pages)]
                 + [page(ATT_DIM, p) for p in range(n_pages)],
        out_specs=seq(ATT_DIM),
        scratch_shapes=[pltpu.VMEM((LANES, n_keys), BF16), pltpu.VMEM((ATT_DIM, n_keys), BF16),
                        pltpu.VMEM((ATT_DIM, n_keys), BF16), pltpu.VMEM((rows, 1), I32)],
    )
    return pl.pallas_call(
        functools.partial(_dsa_sample_kernel, n_pages=n_pages, rows=rows, past=past),
        out_shape=jax.ShapeDtypeStruct((db, rows, ATT_DIM), F32),
        grid_spec=grid_spec,
        compiler_params=_params(1),
        name="dsa_sample",
    )(page_table.reshape(-1), qi, wi, q, kidn, kn, vn,
      *([cache_idx] * n_pages), *([cache_k] * n_pages), *([cache_v] * n_pages))


def _out_kernel(h_ref, att_ref, conv_ref, sgc_ref, sga_ref, g2_ref, sh_ref, sc_ref, g3_ref, gn_ref, gfin_ref,
                wao_ref, wo_ref, wg_ref, wu_ref, wd_ref, y_ref, hn_ref, acc_ref):
    att_o = _dot(att_ref[0], wao_ref[...])
    mix = sgc_ref[0].astype(F32) * conv_ref[0].astype(F32) + sga_ref[0].astype(F32) * att_o
    h2 = h_ref[0] + g2_ref[0] * _dot(mix.astype(BF16), wo_ref[...])
    y_ref[0] = h2
    hn_ref[...] = _rms_mod(h2, gn_ref[...], sc_ref[0], sh_ref[0]).astype(BF16)
    _swiglu_into(hn_ref, wg_ref, wu_ref, wd_ref, acc_ref)
    h3 = y_ref[0] + 0.5 * g3_ref[0] * acc_ref[...]
    ms = jnp.mean(h3 * h3, axis=-1, keepdims=True)
    y_ref[0] = h3 * lax.rsqrt(ms + EPS) * gfin_ref[...]


def _out(h, att, conv, sgc, sga, g2, shift, scale, g3, gn, gfin, wao, wo, wg, wu, wd):
    s, t, _ = h.shape
    tm = min(TOKEN_TILE, t)
    per_token = shift.shape[1] != 1
    mod = _tile_spec(tm, D_MODEL, per_token)
    tok = lambda w: _tile_spec(tm, w, True)
    return pl.pallas_call(
        _out_kernel,
        out_shape=jax.ShapeDtypeStruct(h.shape, F32),
        grid=(s, t // tm),
        in_specs=[tok(D_MODEL), tok(ATT_DIM), tok(D_MODEL), tok(D_MODEL), tok(D_MODEL), mod, mod, mod, mod,
                  _const_spec((1, D_MODEL)), _const_spec((1, D_MODEL)),
                  _const_spec(wao.shape), _const_spec(wo.shape),
                  _const_spec(wg.shape), _const_spec(wu.shape), _const_spec(wd.shape)],
        out_specs=tok(D_MODEL),
        scratch_shapes=[pltpu.VMEM((tm, D_MODEL), BF16), pltpu.VMEM((tm, D_MODEL), F32)],
        compiler_params=_params(2),
        name="out_ffn2",
    )(h, att, conv, sgc, sga, g2, shift, scale, g3, gn, gfin, wao, wo, wg, wu, wd)


def _rope_tables(pos):
    inv = ROPE_THETA ** (-jnp.arange(0, HEAD_DIM, 2, dtype=F32) / HEAD_DIM)
    ang = pos.astype(F32)[:, None] * inv[None, :]
    cos, sin = jnp.cos(ang), jnp.sin(ang)
    cos = jnp.concatenate([cos] * 4, axis=1)
    sin = jnp.concatenate([-sin, sin, -sin, sin], axis=1)
    return cos, sin, cos.T, sin.T


def _ff_weights(wg, wu, wd):
    split_cols = lambda w: w.reshape(D_MODEL, N_FF_CHUNKS, FF_CHUNK).transpose(1, 0, 2).astype(BF16)
    return split_cols(wg), split_cols(wu), wd.reshape(N_FF_CHUNKS, FF_CHUNK, D_MODEL).astype(BF16)


def _in_weights(w_in):
    cuts = [0]
    for width in (D_CONV, D_CONV, ATT_DIM, ATT_DIM, ATT_DIM, IDX_HEADS * IDX_DIM, IDX_DIM, IDX_HEADS,
                  D_MODEL, D_MODEL):
        cuts.append(cuts[-1] + width)
    piece = lambda a, b: w_in[:, cuts[a]:cuts[b]]
    w_qw = jnp.concatenate([piece(5, 6), piece(7, 8), jnp.zeros((D_MODEL, LANES - IDX_HEADS), F32)], axis=1)
    w_ki = piece(6, 7)
    w_kvt = jnp.concatenate([piece(3, 5), w_ki, w_ki], axis=1).T
    return [w.astype(BF16) for w in (piece(0, 2), piece(2, 3), w_qw, piece(8, 10), w_kvt)]


def kernel(x_prompt, x_sample, cache_k, cache_v, cache_idx_k, state_conv, page_table, c_prompt, c_sample,
           w_ada, b_ada, g_ffn1, w1_gate, w1_up, w1_down, g_mix, w_in, w_dw, b_dw, ln_g, ln_b,
           w_conv_out, w_attn_o, w_out, g_ffn2, w2_gate, w2_up, w2_down, g_final):
    bsz, t, _ = x_prompt.shape
    db, s, _ = x_sample.shape
    depth = w_ada.shape[0]
    n_pool = cache_k.shape[1]
    past = page_table.shape[1] * PAGE_SIZE
    q_rows = 8
    assert depth == 1 and t >= CONV_W - 1 and s <= q_rows

    rope_p = _rope_tables(jnp.arange(t))
    rope_s = _rope_tables(jnp.tile(past + jnp.arange(s), db))
    row = lambda v: v.reshape(1, -1)

    c_all = jnp.concatenate([c_prompt, c_sample], axis=0)
    n_c = c_all.shape[0]
    c_all = jnp.pad(c_all, ((0, -n_c % 8), (0, 0)))

    hp = x_prompt
    hs = x_sample.reshape(1, db * s, D_MODEL)
    outs = [[] for _ in range(8)]
    for l in range(depth):
        m = _ada(c_all, w_ada[l], b_ada[l])
        mod_p = [m[:bsz, i * D_MODEL:(i + 1) * D_MODEL].reshape(bsz, 1, D_MODEL) for i in range(N_MOD)]
        mod_s = [jnp.repeat(m[bsz:n_c, i * D_MODEL:(i + 1) * D_MODEL], s, axis=0).reshape(1, db * s, D_MODEL)
                 for i in range(N_MOD)]
        ff1 = _ff_weights(w1_gate[l], w1_up[l], w1_down[l])
        ff2 = _ff_weights(w2_gate[l], w2_up[l], w2_down[l])
        w_ins = _in_weights(w_in[l])
        conv_w = (w_dw[l], row(b_dw[l]), row(ln_g[l]), row(ln_b[l]), w_conv_out[l].astype(BF16))
        wao, wo = w_attn_o[l].astype(BF16), w_out[l].astype(BF16)

        def front(h, mod, rope):
            h = _ffn(h, mod[0], mod[1], mod[2], row(g_ffn1[l]), *ff1)
            return h, _inproj(h, mod[3], mod[4], row(g_mix[l]), *rope, w_ins)

        def back(h, att, conv, sgc, sga, mod):
            return _out(h, att, conv, sgc, sga, mod[5], mod[6], mod[7], mod[8], row(g_ffn2[l]), row(g_final),
                        wao, wo, *ff2)

        hp, (u, q, qi, wi, sgc, sga, kt, kbt, vt, vbt, kidt, kit) = front(hp, mod_p, rope_p)
        conv = _conv_prompt(u, *conv_w)
        att = _dsa_prompt(qi, wi, q, kidt, kbt, vbt)
        hp = back(hp, att, conv, sgc, sga, mod_p)
        heads_last = lambda a: a.reshape(bsz, N_HEADS, HEAD_DIM, t).transpose(0, 3, 1, 2)
        outs[0].append(heads_last(kt))
        outs[1].append(heads_last(vt))
        outs[2].append(kit.transpose(0, 2, 1))
        outs[3].append(u[:, t - (CONV_W - 1):])

        hs, (u, q, qi, wi, sgc, sga, kt, kbt, vt, vbt, kidt, kit) = front(hs, mod_s, rope_s)
        per_seq = lambda a: jnp.pad(a.astype(F32).reshape(db, s, -1), ((0, 0), (0, q_rows - s), (0, 0)))

        def new_keys(a):
            f = a.shape[2]
            a = a[0].transpose(1, 0, 2).reshape(f, db, s).transpose(1, 0, 2)
            return jnp.pad(a, ((0, 0), (0, 0), (0, PAGE_SIZE - s)))

        ext_t = jnp.concatenate([state_conv[l].transpose(1, 0, 2), u.reshape(db, s, D_CONV).transpose(1, 0, 2)],
                                axis=0)
        conv = _conv_sample(ext_t, *conv_w).transpose(1, 0, 2).reshape(1, db * s, D_MODEL)
        att = _dsa_sample(page_table, per_seq(qi), per_seq(wi), per_seq(q),
                          new_keys(kidt), new_keys(kbt), new_keys(vbt),
                          cache_idx_k[l].transpose(0, 2, 1),
                          cache_k[l].transpose(0, 2, 3, 1).reshape(n_pool, ATT_DIM, PAGE_SIZE),
                          cache_v[l].transpose(0, 2, 3, 1).reshape(n_pool, ATT_DIM, PAGE_SIZE))
        att = att[:, :s].reshape(1, db * s, ATT_DIM).astype(BF16)
        hs = back(hs, att, conv, sgc, sga, mod_s)
        heads_last = lambda a: a.reshape(N_HEADS, HEAD_DIM, db, s).transpose(2, 3, 0, 1)
        outs[4].append(heads_last(kt))
        outs[5].append(heads_last(vt))
        outs[6].append(kit.reshape(IDX_DIM, db, s).transpose(1, 2, 0))
        outs[7].append(ext_t[s:].transpose(1, 0, 2))

    return (hp, hs.reshape(db, s, D_MODEL)) + tuple(jnp.stack(o) for o in outs)
```

```python
import functools

import jax
import jax.numpy as jnp
from jax import lax
from jax.experimental import pallas as pl
from jax.experimental.pallas import tpu as pltpu

F32 = jnp.float32
BF16 = jnp.bfloat16
I32 = jnp.int32

D_MODEL = 1024
N_HEADS = 8
HEAD_DIM = 64
ATT_DIM = N_HEADS * HEAD_DIM
IDX_HEADS = 8
IDX_DIM = 64
D_CONV = D_MODEL // 2
CONV_W = 31
D_FF = ((8 * D_MODEL // 3 + 255) // 256) * 256
N_MOD = 9
TOPK_MAX = 256
PAGE_SIZE = 128
ROPE_THETA = 10000.0
EPS = 1e-6
NEG = -1e30

LANES = 128
FF_CHUNK = 256
N_FF_CHUNKS = D_FF // FF_CHUNK
TOKEN_TILE = 512
CONV_HALO = 32
CONV_ROWS = 64
DSA_TQ = 256
DSA_TK = 256
DSA_STRIP = 64
DSA_CANDS = 12
DECODE_BITS = 4
SHIFT_LIMIT = 40.0
VMEM_LIMIT = 56 * 1024 * 1024
INT_MIN = -2 ** 31
INT_MAX = 2 ** 31 - 1
KEY_NEG_INF = INT_MIN + 2 ** 23 - 1


def _dot(a, b):
    return jnp.dot(a, b, preferred_element_type=F32)


def _dot_nt(a, b):
    return lax.dot_general(a, b, (((1,), (1,)), ((), ())), preferred_element_type=F32)


def _sigmoid(x):
    return jax.nn.sigmoid(x)


def _rms_mod(x, g, scale, shift):
    ms = jnp.mean(x * x, axis=-1, keepdims=True)
    return (x * lax.rsqrt(ms + EPS)) * g * (1.0 + scale) + shift


def _key_to_f32(key):
    return pltpu.bitcast(key ^ ((key >> 31) & INT_MAX), F32)


def _const_spec(shape):
    nd = len(shape)
    return pl.BlockSpec(shape, lambda *_: (0,) * nd, pipeline_mode=pl.Buffered(1))


def _params(n_grid):
    return pltpu.CompilerParams(dimension_semantics=("arbitrary",) * n_grid, vmem_limit_bytes=VMEM_LIMIT)


def _ada_kernel(c_ref, w_ref, b_ref, o_ref):
    c = c_ref[...]
    s = (c * _sigmoid(c)).astype(BF16)
    o_ref[...] = _dot(s, w_ref[...].astype(BF16)) + b_ref[...]


def _ada(c, w, b):
    rows = c.shape[0]
    nb = D_MODEL
    return pl.pallas_call(
        _ada_kernel,
        out_shape=jax.ShapeDtypeStruct((rows, N_MOD * D_MODEL), F32),
        grid=(N_MOD * D_MODEL // nb,),
        in_specs=[pl.BlockSpec((rows, D_MODEL), lambda j: (0, 0)),
                  pl.BlockSpec((D_MODEL, nb), lambda j: (0, j)),
                  pl.BlockSpec((1, nb), lambda j: (0, j))],
        out_specs=pl.BlockSpec((rows, nb), lambda j: (0, j)),
        compiler_params=_params(1),
        name="ada",
    )(c, w, b.reshape(1, -1))


def _swiglu_into(hn_ref, wg_ref, wu_ref, wd_ref, acc_ref):
    acc_ref[...] = jnp.zeros_like(acc_ref)

    def body(i, carry):
        hn = hn_ref[...]
        a = _dot(hn, wg_ref[i])
        b = _dot(hn, wu_ref[i])
        act = (a * _sigmoid(a) * b).astype(BF16)
        acc_ref[...] += _dot(act, wd_ref[i])
        return carry

    lax.fori_loop(0, N_FF_CHUNKS, body, 0)


def _ffn_kernel(x_ref, sh_ref, sc_ref, gt_ref, g_ref, wg_ref, wu_ref, wd_ref, o_ref, hn_ref, acc_ref):
    x = x_ref[0]
    hn_ref[...] = _rms_mod(x, g_ref[...], sc_ref[0], sh_ref[0]).astype(BF16)
    _swiglu_into(hn_ref, wg_ref, wu_ref, wd_ref, acc_ref)
    o_ref[0] = x + 0.5 * gt_ref[0] * acc_ref[...]


def _tile_spec(tm, width, per_token):
    if per_token:
        return pl.BlockSpec((1, tm, width), lambda b, i: (b, i, 0))
    return pl.BlockSpec((1, 1, width), lambda b, i: (b, 0, 0))


def _ffn(x, shift, scale, gate, g, wg, wu, wd):
    s, t, _ = x.shape
    tm = min(TOKEN_TILE, t)
    per_token = shift.shape[1] != 1
    tok = _tile_spec(tm, D_MODEL, True)
    mod = _tile_spec(tm, D_MODEL, per_token)
    return pl.pallas_call(
        _ffn_kernel,
        out_shape=jax.ShapeDtypeStruct(x.shape, F32),
        grid=(s, t // tm),
        in_specs=[tok, mod, mod, mod, _const_spec((1, D_MODEL)),
                  _const_spec(wg.shape), _const_spec(wu.shape), _const_spec(wd.shape)],
        out_specs=tok,
        scratch_shapes=[pltpu.VMEM((tm, D_MODEL), BF16), pltpu.VMEM((tm, D_MODEL), F32)],
        compiler_params=_params(2),
        name="ffn1",
    )(x, shift, scale, gate, g, wg, wu, wd)


def _rope(x, cos, sin, lo32):
    sw = jnp.where(lo32, pltpu.roll(x, 96, 1), pltpu.roll(x, 32, 1))
    return x * cos + sw * sin


def _rope_t(x, cos, sin):
    h = HEAD_DIM // 2
    sw = jnp.concatenate([x[h:2 * h], x[0:h], x[3 * h:4 * h], x[2 * h:3 * h]], axis=0)
    return x * cos + sw * sin


def _inproj_kernel(h_ref, sh_ref, sc_ref, g_ref, cos_ref, sin_ref, cost_ref, sint_ref,
                   wglu_ref, wq_ref, wqw_ref, wgt_ref, wkvt_ref,
                   u_ref, q_ref, qi_ref, wi_ref, sgc_ref, sga_ref,
                   kt_ref, kbt_ref, vt_ref, vbt_ref, kidt_ref, kit_ref, kn_ref, n_ref, *, tm, tk):
    n_ref[...] = _rms_mod(h_ref[0], g_ref[...], sc_ref[0], sh_ref[0]).astype(BF16)
    lane = lax.broadcasted_iota(I32, (1, LANES), 1)
    lo32 = (lane & (HEAD_DIM - 1)) < HEAD_DIM // 2
    rope = functools.partial(_rope, cos=cos_ref[...], sin=sin_ref[...], lo32=lo32)
    n_q = ATT_DIM // LANES

    z = _dot(n_ref[...], wglu_ref[...])
    u_ref[0] = z[:, :D_CONV] * _sigmoid(z[:, D_CONV:])

    z = _dot(n_ref[...], wq_ref[...])
    for s in range(n_q):
        sl = slice(s * LANES, (s + 1) * LANES)
        q_ref[0, :, sl] = (rope(z[:, sl]) * HEAD_DIM ** -0.5).astype(BF16)

    z = _dot(n_ref[...], wqw_ref[...])
    for s in range(n_q):
        sl = slice(s * LANES, (s + 1) * LANES)
        qi_ref[0, :, sl] = rope(z[:, sl]).astype(BF16)
    wi_ref[0] = z[:, ATT_DIM:ATT_DIM + IDX_HEADS] * (IDX_HEADS * IDX_DIM) ** -0.5

    z = _dot(n_ref[...], wgt_ref[...])
    sgc_ref[0] = _sigmoid(z[:, :D_MODEL]).astype(BF16)
    sga_ref[0] = _sigmoid(z[:, D_MODEL:]).astype(BF16)

    zt = _dot_nt(wkvt_ref[...], n_ref[...])
    cost, sint = cost_ref[...], sint_ref[...]
    chunks = [slice(j * tk, (j + 1) * tk) for j in range(tm // tk)]
    norms = []
    for s in range(n_q):
        rows = slice(s * LANES, (s + 1) * LANES)
        xk = _rope_t(zt[rows], cost, sint)
        kt_ref[0, rows, :] = xk
        xkb = xk.astype(BF16)
        for j, ch in enumerate(chunks):
            kbt_ref[0, j, rows, :] = xkb[:, ch]
        sq = xkb.astype(F32)
        sq = sq * sq
        norms += [jnp.sum(sq[:HEAD_DIM], axis=0, keepdims=True), jnp.sum(sq[HEAD_DIM:], axis=0, keepdims=True)]
    kn_ref[0] = jnp.concatenate(norms, axis=0)
    zv = zt[ATT_DIM:2 * ATT_DIM]
    vt_ref[0] = zv
    for j, ch in enumerate(chunks):
        vbt_ref[0, j] = zv[:, ch].astype(BF16)
    kid = _rope_t(zt[2 * ATT_DIM:], cost, sint)
    kit_ref[0] = kid[:IDX_DIM]
    for j, ch in enumerate(chunks):
        kidt_ref[0, j] = kid[:, ch].astype(BF16)


def _inproj(h, shift, scale, g, cos, sin, cost, sint, ws):
    s, t, _ = h.shape
    tm = min(TOKEN_TILE, t)
    tk = DSA_TK
    per_token = shift.shape[1] != 1
    mod = _tile_spec(tm, D_MODEL, per_token)
    tab = pl.BlockSpec((tm, LANES), lambda b, i: (i, 0))
    tabt = pl.BlockSpec((LANES, tm), lambda b, i: (0, i))
    tok = [(D_CONV, F32), (ATT_DIM, BF16), (IDX_HEADS * IDX_DIM, BF16), (IDX_HEADS, F32),
           (D_MODEL, BF16), (D_MODEL, BF16)]
    feat = lambda rows, dt: (jax.ShapeDtypeStruct((s, rows, t), dt),
                             pl.BlockSpec((1, rows, tm), lambda b, i: (b, 0, i)))
    chunked = lambda rows: (jax.ShapeDtypeStruct((s, t // tk, rows, tk), BF16),
                            pl.BlockSpec((1, tm // tk, rows, tk), lambda b, i: (b, i, 0, 0)))
    key_side = [feat(ATT_DIM, F32), chunked(ATT_DIM), feat(ATT_DIM, F32), chunked(ATT_DIM),
                chunked(LANES), feat(IDX_DIM, F32), feat(N_HEADS, F32)]
    return pl.pallas_call(
        functools.partial(_inproj_kernel, tm=tm, tk=tk),
        out_shape=[jax.ShapeDtypeStruct((s, t, w), dt) for w, dt in tok] + [sd for sd, _ in key_side],
        grid=(s, t // tm),
        in_specs=[_tile_spec(tm, D_MODEL, True), mod, mod, _const_spec((1, D_MODEL)), tab, tab, tabt, tabt]
                 + [_const_spec(w.shape) for w in ws],
        out_specs=[_tile_spec(tm, w, True) for w, _ in tok] + [sp for _, sp in key_side],
        scratch_shapes=[pltpu.VMEM((tm, D_MODEL), BF16)],
        compiler_params=_params(2),
        name="inproj",
    )(h, shift, scale, g, cos, sin, cost, sint, *ws)


def _ln_silu(y, g, b):
    mu = jnp.mean(y, axis=-1, keepdims=True)
    yc = y - mu
    var = jnp.mean(yc * yc, axis=-1, keepdims=True)
    y = yc * lax.rsqrt(var + EPS) * g + b
    return y * _sigmoid(y)


def _conv_kernel(u_ref, up_ref, wdw_ref, bdw_ref, lng_ref, lnb_ref, wco_ref, o_ref, ext_ref, y_ref, *, tm):
    i = pl.program_id(1)
    ext_ref[0:CONV_HALO, :] = jnp.where(i > 0, up_ref[0], 0.0)
    ext_ref[CONV_HALO:, :] = u_ref[0]
    off = CONV_HALO - (CONV_W - 1)
    for r in range(tm // CONV_ROWS):
        acc = jnp.broadcast_to(bdw_ref[...], (CONV_ROWS, D_CONV))
        for j in range(CONV_W):
            lo = r * CONV_ROWS + off + j
            acc = acc + wdw_ref[j:j + 1, :] * ext_ref[lo:lo + CONV_ROWS, :]
        y_ref[r * CONV_ROWS:(r + 1) * CONV_ROWS, :] = _ln_silu(acc, lng_ref[...], lnb_ref[...]).astype(BF16)
    o_ref[0] = _dot(y_ref[...], wco_ref[...]).astype(BF16)


def _conv_prompt(u, wdw, bdw, lng, lnb, wco):
    b, t, _ = u.shape
    tm = min(TOKEN_TILE, t)
    halo_blocks = tm // CONV_HALO
    return pl.pallas_call(
        functools.partial(_conv_kernel, tm=tm),
        out_shape=jax.ShapeDtypeStruct((b, t, D_MODEL), BF16),
        grid=(b, t // tm),
        in_specs=[pl.BlockSpec((1, tm, D_CONV), lambda s, i: (s, i, 0)),
                  pl.BlockSpec((1, CONV_HALO, D_CONV), lambda s, i: (s, jnp.maximum(i * halo_blocks - 1, 0), 0)),
                  _const_spec(wdw.shape), _const_spec(bdw.shape), _const_spec(lng.shape), _const_spec(lnb.shape),
                  _const_spec(wco.shape)],
        out_specs=pl.BlockSpec((1, tm, D_MODEL), lambda s, i: (s, i, 0)),
        scratch_shapes=[pltpu.VMEM((tm + CONV_HALO, D_CONV), F32), pltpu.VMEM((tm, D_CONV), BF16)],
        compiler_params=_params(2),
        name="conv_prompt",
    )(u, u, wdw, bdw, lng, lnb, wco)


def _conv_sample_kernel(ext_ref, wdw_ref, bdw_ref, lng_ref, lnb_ref, wco_ref, o_ref, *, steps, rows):
    for s in range(steps):
        acc = jnp.broadcast_to(bdw_ref[...], (rows, D_CONV))
        for j in range(CONV_W):
            acc = acc + wdw_ref[j:j + 1, :] * ext_ref[s + j]
        y = _ln_silu(acc, lng_ref[...], lnb_ref[...]).astype(BF16)
        o_ref[s] = _dot(y, wco_ref[...]).astype(BF16)


def _conv_sample(ext_t, wdw, bdw, lng, lnb, wco):
    n, rows, _ = ext_t.shape
    steps = n - (CONV_W - 1)
    return pl.pallas_call(
        functools.partial(_conv_sample_kernel, steps=steps, rows=rows),
        out_shape=jax.ShapeDtypeStruct((steps, rows, D_MODEL), BF16),
        grid=(1,),
        in_specs=[_const_spec(ext_t.shape), _const_spec(wdw.shape), _const_spec(bdw.shape),
                  _const_spec(lng.shape), _const_spec(lnb.shape), _const_spec(wco.shape)],
        out_specs=pl.BlockSpec((steps, rows, D_MODEL), lambda i: (0, 0, 0)),
        compiler_params=_params(1),
        name="conv_sample",
    )(ext_t, wdw, bdw, lng, lnb, wco)


def _count_term(kind, scores, kpos, thr, pos):
    if kind == "ge":
        return jnp.where(scores >= thr, 1, 0)
    if kind == "gt":
        return jnp.where(scores > thr, 1, 0)
    if kind == "eq":
        return jnp.where(scores == thr, 1, 0)
    return jnp.where(scores == thr, jnp.where(kpos < pos, 1, 0), 0)


def _select_threshold(count, rows, bits=1):
    assert 32 % bits == 0

    def pass_body(p, key):
        shift = 32 - bits * (p + 1)
        digit = jnp.zeros((rows, 1), I32)
        for i in range(1, 2 ** bits):
            cand = key + lax.shift_left(jnp.int32(i), shift)
            digit = digit + jnp.where(count("ge", _key_to_f32(cand), None) >= TOPK_MAX, 1, 0)
        return key + lax.shift_left(digit, shift)

    key = lax.fori_loop(0, 32 // bits, pass_body, jnp.full((rows, 1), INT_MIN, I32))
    thr = jnp.where(key < KEY_NEG_INF, -jnp.inf, _key_to_f32(key))
    need = TOPK_MAX - count("gt", thr, None)
    n_eq = count("eq", thr, None)
    return thr, need, n_eq


def _tie_position(count, thr, need, n_bits):
    def bit_body(b, j):
        cand = j + lax.shift_left(jnp.int32(1), n_bits - 1 - b)
        return jnp.where(count("tie", thr, cand) < need, cand, j)

    return lax.fori_loop(0, n_bits, bit_body, jnp.zeros(thr.shape, I32))


def _bias_from_scores(scores, kpos, qpos, thr, last_tie):
    t1 = jnp.where(kpos <= last_tie, 0.0, NEG)
    t2 = jnp.where(scores == thr, t1, NEG)
    t3 = jnp.where(scores > thr, 0.0, t2)
    return jnp.where(kpos <= qpos, t3, NEG)


def _split_heads(x, lo):
    zero = jnp.zeros_like(x)
    return jnp.where(lo, x, zero), jnp.where(lo, zero, x)


def _dsa_prompt_kernel(qi_ref, wi_ref, q_ref, kid_ref, kb_ref, vb_ref, kn_ref, o_ref,
                       qim_ref, qm_ref, wib_ref, sc_ref, s8_ref, s_ref, p_ref, cand_ref, drop_ref, sel_ref,
                       thr_ref, pos_ref, tie_ref, shift_ref, m_ref, l_ref, al_ref, acc_ref, *, tq, tk, n_bits):
    st = DSA_STRIP
    n_st = tq // st
    n_lt = tk // LANES
    q0 = pl.program_id(1) * tq
    n_ch = pl.program_id(1) + 1
    lo = lax.broadcasted_iota(I32, (tq, LANES), 1) < HEAD_DIM
    strips = [slice(r * st, (r + 1) * st) for r in range(n_st)]
    tiles = [slice(t * LANES, (t + 1) * LANES) for t in range(n_lt)]
    lane_s = lax.broadcasted_iota(I32, (st, LANES), 1)
    row_s = lax.broadcasted_iota(I32, (st, LANES), 0)

    for pr in range(N_HEADS // 2):
        sl = slice(pr * LANES, (pr + 1) * LANES)
        qim_ref[2 * pr], qim_ref[2 * pr + 1] = _split_heads(qi_ref[0, :, sl], lo)
        qm_ref[pr, 0:tq], qm_ref[pr, tq:2 * tq] = _split_heads(q_ref[0, :, sl], lo)
    for h in range(IDX_HEADS):
        wib_ref[h] = jnp.broadcast_to(wi_ref[0, :, h:h + 1], (tq, tk))
    cand_ref[...] = jnp.full(cand_ref.shape, -jnp.inf, F32)
    drop_ref[...] = jnp.full(drop_ref.shape, -jnp.inf, F32)

    def score_chunk(c, carry):
        kc = kid_ref[0, c]
        for h in range(IDX_HEADS):
            s8_ref[h] = _dot(qim_ref[h], kc)
        for r, rows in enumerate(strips):
            acc = jnp.zeros((st, tk), F32)
            for h in range(IDX_HEADS):
                acc = acc + wib_ref[h, rows, :] * jnp.maximum(s8_ref[h, rows, :], 0.0)
            xs = []
            for t, tl in enumerate(tiles):
                visible = lane_s + (c * tk + t * LANES) <= row_s + (q0 + r * st)
                xs.append(jnp.where(visible, acc[:, tl], NEG))
                sc_ref[c, rows, tl] = xs[t]
            for j in range(DSA_CANDS):
                kept = cand_ref[j, rows, :]
                for t in range(n_lt):
                    kept, xs[t] = jnp.maximum(kept, xs[t]), jnp.minimum(kept, xs[t])
                cand_ref[j, rows, :] = kept
            dropped = drop_ref[rows, :]
            for x in xs:
                dropped = jnp.maximum(dropped, x)
            drop_ref[rows, :] = dropped
        return carry

    lax.fori_loop(0, n_ch, score_chunk, 0)

    def count(kind, thr, pos):
        thr_ref[...] = jnp.broadcast_to(thr, (tq, LANES))
        if pos is not None:
            pos_ref[...] = jnp.broadcast_to(pos, (tq, LANES))

        def body(c, accs):
            out = []
            for r, rows in enumerate(strips):
                thr_b = thr_ref[rows, :]
                pos_b = pos_ref[rows, :] if pos is not None else None
                a = accs[r]
                for t, tl in enumerate(tiles):
                    kpos = lane_s + (c * tk + t * LANES)
                    a = a + _count_term(kind, sc_ref[c, rows, tl], kpos, thr_b, pos_b)
                out.append(a)
            return tuple(out)

        accs = lax.fori_loop(0, n_ch, body, tuple(jnp.zeros((st, LANES), I32) for _ in strips))
        return jnp.concatenate([jnp.sum(a.astype(F32), axis=1, keepdims=True) for a in accs], axis=0)

    def count_kept(kind, thr, pos):
        outs = []
        for rows in strips:
            thr_b = jnp.broadcast_to(thr[rows], (st, LANES))
            a = jnp.zeros((st, LANES), I32)
            for j in range(DSA_CANDS):
                a = a + _count_term(kind, cand_ref[j, rows, :], None, thr_b, None)
            outs.append(jnp.sum(a.astype(F32), axis=1, keepdims=True))
        return jnp.concatenate(outs, axis=0)

    for v, x in enumerate(_select_threshold(count_kept, tq)):
        sel_ref[v] = jnp.broadcast_to(x, (tq, LANES))
    dropped = drop_ref[...]
    unsafe = jnp.where(dropped > -jnp.inf, jnp.where(dropped >= sel_ref[0], 1.0, 0.0), 0.0)

    @pl.when(jnp.max(unsafe) > 0.5)
    def _():
        for v, x in enumerate(_select_threshold(count, tq)):
            sel_ref[v] = jnp.broadcast_to(x, (tq, LANES))

    thr, need, n_eq = (sel_ref[v, :, 0:1] for v in range(3))
    tie_ref[...] = jnp.full((tq, LANES), INT_MAX, I32)
    split = n_eq > need

    @pl.when(jnp.max(jnp.where(split, 1.0, 0.0)) > 0.5)
    def _():
        j = _tie_position(count, thr, need, n_bits)
        tie_ref[...] = jnp.broadcast_to(jnp.where(split, j, INT_MAX), (tq, LANES))

    thr_ref[...] = jnp.broadcast_to(thr, (tq, LANES))

    def to_bias(c, carry):
        for r, rows in enumerate(strips):
            thr_b, tie_b = thr_ref[rows, :], tie_ref[rows, :]
            qpos = row_s + (q0 + r * st)
            for t, tl in enumerate(tiles):
                kpos = lane_s + (c * tk + t * LANES)
                sc_ref[c, rows, tl] = _bias_from_scores(sc_ref[c, rows, tl], kpos, qpos, thr_b, tie_b)
        return carry

    lax.fori_loop(0, n_ch, to_bias, 0)

    l_ref[...] = jnp.zeros(l_ref.shape, F32)
    acc_ref[...] = jnp.zeros(acc_ref.shape, F32)

    kn = kn_ref[0]
    visible = lax.broadcasted_iota(I32, kn.shape, 1) < q0 + tq
    kmax = jnp.sqrt(jnp.max(jnp.where(visible, kn, 0.0), axis=1, keepdims=True))
    for h in range(N_HEADS):
        qh = qm_ref[h // 2, (h % 2) * tq:(h % 2 + 1) * tq, :].astype(F32)
        qn = jnp.sqrt(jnp.sum(qh * qh, axis=1, keepdims=True))
        shift_ref[h] = jnp.broadcast_to(qn * kmax[h:h + 1, :], (tq, LANES))
    bounded = jnp.max(shift_ref[...]) <= SHIFT_LIMIT

    def attend_chunk_bounded(c, carry):
        for pr in range(N_HEADS // 2):
            feat = slice(pr * LANES, (pr + 1) * LANES)
            s_ref[pr] = _dot(qm_ref[pr], kb_ref[0, c, feat, :])
            for e in range(2):
                h = 2 * pr + e
                for r, rows in enumerate(strips):
                    srows = slice(e * tq + r * st, e * tq + (r + 1) * st)
                    shift = shift_ref[h, rows, :]
                    psum = l_ref[h, rows, :]
                    for tl in tiles:
                        p = jnp.exp(s_ref[pr, srows, tl] + sc_ref[c, rows, tl] - shift)
                        psum = psum + p
                        p_ref[pr, srows, tl] = p.astype(BF16)
                    l_ref[h, rows, :] = psum
            pv = _dot_nt(p_ref[pr], vb_ref[0, c, feat, :])
            acc_ref[pr] = acc_ref[pr] + jnp.where(lo, pv[:tq], pv[tq:])
        return carry

    @pl.when(bounded)
    def _():
        lax.fori_loop(0, n_ch, attend_chunk_bounded, 0)

    def attend_chunk(c, carry):
        for pr in range(N_HEADS // 2):
            feat = slice(pr * LANES, (pr + 1) * LANES)
            s_ref[pr] = _dot(qm_ref[pr], kb_ref[0, c, feat, :])
            for e in range(2):
                h = 2 * pr + e
                for r, rows in enumerate(strips):
                    srows = slice(e * tq + r * st, e * tq + (r + 1) * st)
                    s = [s_ref[pr, srows, tl] + sc_ref[c, rows, tl] for tl in tiles]
                    mx = s[0]
                    for x in s[1:]:
                        mx = jnp.maximum(mx, x)
                    m_prev = m_ref[h, rows, :]
                    m_new = jnp.maximum(m_prev, jnp.max(mx, axis=1, keepdims=True))
                    alpha = jnp.exp(m_prev - m_new)
                    p = [jnp.exp(x - m_new) for x in s]
                    psum = p[0]
                    for x in p[1:]:
                        psum = psum + x
                    l_ref[h, rows, :] = alpha * l_ref[h, rows, :] + psum
                    m_ref[h, rows, :] = m_new
                    al_ref[e, rows, :] = alpha
                    for x, tl in zip(p, tiles):
                        p_ref[pr, srows, tl] = x.astype(BF16)
            pv = _dot_nt(p_ref[pr], vb_ref[0, c, feat, :])
            a = acc_ref[pr]
            acc_ref[pr] = jnp.where(lo, a * al_ref[0] + pv[:tq], a * al_ref[1] + pv[tq:])
        return carry

    @pl.when(jnp.logical_not(bounded))
    def _():
        m_ref[...] = jnp.full(m_ref.shape, NEG, F32)
        lax.fori_loop(0, n_ch, attend_chunk, 0)

    for pr in range(N_HEADS // 2):
        l_e = jnp.sum(l_ref[2 * pr], axis=1, keepdims=True)
        l_o = jnp.sum(l_ref[2 * pr + 1], axis=1, keepdims=True)
        o_ref[0, :, pr * LANES:(pr + 1) * LANES] = (acc_ref[pr] * jnp.where(lo, 1.0 / l_e, 1.0 / l_o)).astype(BF16)


def _dsa_prompt(qi, wi, q, kidt, kbt, vbt, knorm):
    b, t, _ = q.shape
    tq, tk = DSA_TQ, DSA_TK
    assert tq == tk and t % tq == 0
    n_bits = max(1, (t - 1).bit_length())
    qspec = lambda w: pl.BlockSpec((1, tq, w), lambda s, i: (s, i, 0))
    kspec = lambda rows: pl.BlockSpec((1, t // tk, rows, tk), lambda s, i: (s, 0, 0, 0),
                                      pipeline_mode=pl.Buffered(1))
    return pl.pallas_call(
        functools.partial(_dsa_prompt_kernel, tq=tq, tk=tk, n_bits=n_bits),
        out_shape=jax.ShapeDtypeStruct((b, t, ATT_DIM), BF16),
        grid=(b, t // tq),
        in_specs=[qspec(IDX_HEADS * IDX_DIM), qspec(IDX_HEADS), qspec(ATT_DIM),
                  kspec(LANES), kspec(ATT_DIM), kspec(ATT_DIM),
                  pl.BlockSpec((1, N_HEADS, t), lambda s, i: (s, 0, 0), pipeline_mode=pl.Buffered(1))],
        out_specs=qspec(ATT_DIM),
        scratch_shapes=[pltpu.VMEM((IDX_HEADS, tq, LANES), BF16),
                        pltpu.VMEM((N_HEADS // 2, 2 * tq, LANES), BF16),
                        pltpu.VMEM((IDX_HEADS, tq, tk), F32),
                        pltpu.VMEM((t // tk, tq, tk), F32),
                        pltpu.VMEM((IDX_HEADS, tq, tk), F32),
                        pltpu.VMEM((N_HEADS // 2, 2 * tq, tk), F32),
                        pltpu.VMEM((N_HEADS // 2, 2 * tq, tk), BF16),
                        pltpu.VMEM((DSA_CANDS, tq, LANES), F32),
                        pltpu.VMEM((tq, LANES), F32),
                        pltpu.VMEM((3, tq, LANES), F32),
                        pltpu.VMEM((tq, LANES), F32),
                        pltpu.VMEM((tq, LANES), I32),
                        pltpu.VMEM((tq, LANES), I32),
                        pltpu.VMEM((N_HEADS, tq, LANES), F32),
                        pltpu.VMEM((N_HEADS, tq, LANES), F32),
                        pltpu.VMEM((N_HEADS, tq, LANES), F32),
                        pltpu.VMEM((2, tq, LANES), F32),
                        pltpu.VMEM((N_HEADS // 2, tq, LANES), F32)],
        compiler_params=_params(2),
        name="dsa_prompt",
    )(qi, wi, q, kidt, kbt, vbt, knorm)


def _dsa_sample_kernel(pt_ref, qi_ref, wi_ref, q_ref, kidn_ref, kn_ref, vn_ref, *rest, n_pages, rows, past):
    idx_refs = rest[:n_pages]
    k_refs = rest[n_pages:2 * n_pages]
    v_refs = rest[2 * n_pages:3 * n_pages]
    o_ref, kid_all, k_all, v_all, tie_ref = rest[3 * n_pages:]
    n_keys = past + PAGE_SIZE

    for p in range(n_pages):
        sl = slice(p * PAGE_SIZE, (p + 1) * PAGE_SIZE)
        x = idx_refs[p][0]
        kid_all[:, sl] = jnp.concatenate([x, x], axis=0).astype(BF16)
        k_all[:, sl] = k_refs[p][0].astype(BF16)
        v_all[:, sl] = v_refs[p][0].astype(BF16)
    tail = slice(past, n_keys)
    kid_all[:, tail] = kidn_ref[0]
    k_all[:, tail] = kn_ref[0]
    v_all[:, tail] = vn_ref[0]

    lo = lax.broadcasted_iota(I32, (rows, LANES), 1) < HEAD_DIM
    qi = qi_ref[0]
    parts = []
    for pr in range(IDX_HEADS // 2):
        parts.extend(_split_heads(qi[:, pr * LANES:(pr + 1) * LANES], lo))
    s_all = _dot(jnp.concatenate(parts, axis=0).astype(BF16), kid_all[...])
    wi = wi_ref[0]
    acc = jnp.zeros((rows, n_keys), F32)
    for h in range(IDX_HEADS):
        acc = acc + wi[:, h:h + 1] * jnp.maximum(s_all[h * rows:(h + 1) * rows], 0.0)
    kpos = lax.broadcasted_iota(I32, (rows, n_keys), 1)
    qpos = lax.broadcasted_iota(I32, (rows, n_keys), 0) + past
    scores = jnp.where(kpos <= qpos, acc, NEG)

    def count(kind, thr, pos):
        return jnp.sum(_count_term(kind, scores, kpos, thr, pos).astype(F32), axis=1, keepdims=True)

    n_bits = (n_keys - 1).bit_length()
    thr, need, n_eq = _select_threshold(count, rows, DECODE_BITS)
    tie_ref[...] = jnp.full((rows, 1), INT_MAX, I32)
    split = n_eq > need

    @pl.when(jnp.max(jnp.where(split, 1.0, 0.0)) > 0.5)
    def _():
        tie_ref[...] = jnp.where(split, _tie_position(count, thr, need, n_bits), INT_MAX)

    bias = _bias_from_scores(scores, kpos, qpos, thr, tie_ref[...])
    bias2 = jnp.concatenate([bias, bias], axis=0)

    q = q_ref[0]
    for pr in range(N_HEADS // 2):
        sl = slice(pr * LANES, (pr + 1) * LANES)
        lhs = jnp.concatenate(_split_heads(q[:, sl], lo), axis=0).astype(BF16)
        s = _dot(lhs, k_all[sl, :]) + bias2
        p = jnp.exp(s - jnp.max(s, axis=1, keepdims=True))
        l = jnp.sum(p, axis=1, keepdims=True)
        pv = _dot_nt(p.astype(BF16), v_all[sl, :]) / l
        o_ref[0, :, sl] = jnp.where(lo, pv[:rows], pv[rows:])


def _dsa_sample(page_table, qi, wi, q, kidn, kn, vn, cache_idx, cache_k, cache_v):
    db, rows, _ = q.shape
    n_pages = page_table.shape[1]
    past = n_pages * PAGE_SIZE
    n_keys = past + PAGE_SIZE
    seq = lambda w: pl.BlockSpec((1, rows, w), lambda s, pt: (s, 0, 0))
    new = lambda f: pl.BlockSpec((1, f, PAGE_SIZE), lambda s, pt: (s, 0, 0))

    def page(f, p):
        return pl.BlockSpec((1, f, PAGE_SIZE), lambda s, pt: (pt[s * n_pages + p], 0, 0))

    grid_spec = pltpu.PrefetchScalarGridSpec(
        num_scalar_prefetch=1,
        grid=(db,),
        in_specs=[seq(IDX_HEADS * IDX_DIM), seq(IDX_HEADS), seq(ATT_DIM), new(LANES), new(ATT_DIM), new(ATT_DIM)]
                 + [page(IDX_DIM, p) for p in range(n_pages)]
                 + [page(ATT_DIM, p) for p in range(n_pages)]
                 + [page(ATT_DIM, p) for p in range(n_pages)],
        out_specs=seq(ATT_DIM),
        scratch_shapes=[pltpu.VMEM((LANES, n_keys), BF16), pltpu.VMEM((ATT_DIM, n_keys), BF16),
                        pltpu.VMEM((ATT_DIM, n_keys), BF16), pltpu.VMEM((rows, 1), I32)],
    )
    return pl.pallas_call(
        functools.partial(_dsa_sample_kernel, n_pages=n_pages, rows=rows, past=past),
        out_shape=jax.ShapeDtypeStruct((db, rows, ATT_DIM), F32),
        grid_spec=grid_spec,
        compiler_params=_params(1),
        name="dsa_sample",
    )(page_table.reshape(-1), qi, wi, q, kidn, kn, vn,
      *([cache_idx] * n_pages), *([cache_k] * n_pages), *([cache_v] * n_pages))


def _out_kernel(h_ref, att_ref, conv_ref, sgc_ref, sga_ref, g2_ref, sh_ref, sc_ref, g3_ref, gn_ref, gfin_ref,
                wao_ref, wo_ref, wg_ref, wu_ref, wd_ref, y_ref, hn_ref, acc_ref):
    att_o = _dot(att_ref[0], wao_ref[...])
    mix = sgc_ref[0].astype(F32) * conv_ref[0].astype(F32) + sga_ref[0].astype(F32) * att_o
    h2 = h_ref[0] + g2_ref[0] * _dot(mix.astype(BF16), wo_ref[...])
    y_ref[0] = h2
    hn_ref[...] = _rms_mod(h2, gn_ref[...], sc_ref[0], sh_ref[0]).astype(BF16)
    _swiglu_into(hn_ref, wg_ref, wu_ref, wd_ref, acc_ref)
    h3 = y_ref[0] + 0.5 * g3_ref[0] * acc_ref[...]
    ms = jnp.mean(h3 * h3, axis=-1, keepdims=True)
    y_ref[0] = h3 * lax.rsqrt(ms + EPS) * gfin_ref[...]


def _out(h, att, conv, sgc, sga, g2, shift, scale, g3, gn, gfin, wao, wo, wg, wu, wd):
    s, t, _ = h.shape
    tm = min(TOKEN_TILE, t)
    per_token = shift.shape[1] != 1
    mod = _tile_spec(tm, D_MODEL, per_token)
    tok = lambda w: _tile_spec(tm, w, True)
    return pl.pallas_call(
        _out_kernel,
        out_shape=jax.ShapeDtypeStruct(h.shape, F32),
        grid=(s, t // tm),
        in_specs=[tok(D_MODEL), tok(ATT_DIM), tok(D_MODEL), tok(D_MODEL), tok(D_MODEL), mod, mod, mod, mod,
                  _const_spec((1, D_MODEL)), _const_spec((1, D_MODEL)),
                  _const_spec(wao.shape), _const_spec(wo.shape),
                  _const_spec(wg.shape), _const_spec(wu.shape), _const_spec(wd.shape)],
        out_specs=tok(D_MODEL),
        scratch_shapes=[pltpu.VMEM((tm, D_MODEL), BF16), pltpu.VMEM((tm, D_MODEL), F32)],
        compiler_params=_params(2),
        name="out_ffn2",
    )(h, att, conv, sgc, sga, g2, shift, scale, g3, gn, gfin, wao, wo, wg, wu, wd)


def _rope_tables(pos):
    inv = ROPE_THETA ** (-jnp.arange(0, HEAD_DIM, 2, dtype=F32) / HEAD_DIM)
    ang = pos.astype(F32)[:, None] * inv[None, :]
    cos, sin = jnp.cos(ang), jnp.sin(ang)
    cos = jnp.concatenate([cos] * 4, axis=1)
    sin = jnp.concatenate([-sin, sin, -sin, sin], axis=1)
    return cos, sin, cos.T, sin.T


def _ff_weights(wg, wu, wd):
    split_cols = lambda w: w.reshape(D_MODEL, N_FF_CHUNKS, FF_CHUNK).transpose(1, 0, 2).astype(BF16)
    return split_cols(wg), split_cols(wu), wd.reshape(N_FF_CHUNKS, FF_CHUNK, D_MODEL).astype(BF16)


def _in_weights(w_in):
    cuts = [0]
    for width in (D_CONV, D_CONV, ATT_DIM, ATT_DIM, ATT_DIM, IDX_HEADS * IDX_DIM, IDX_DIM, IDX_HEADS,
                  D_MODEL, D_MODEL):
        cuts.append(cuts[-1] + width)
    piece = lambda a, b: w_in[:, cuts[a]:cuts[b]]
    w_qw = jnp.concatenate([piece(5, 6), piece(7, 8), jnp.zeros((D_MODEL, LANES - IDX_HEADS), F32)], axis=1)
    w_ki = piece(6, 7)
    w_kvt = jnp.concatenate([piece(3, 5), w_ki, w_ki], axis=1).T
    return [w.astype(BF16) for w in (piece(0, 2), piece(2, 3), w_qw, piece(8, 10), w_kvt)]


def kernel(x_prompt, x_sample, cache_k, cache_v, cache_idx_k, state_conv, page_table, c_prompt, c_sample,
           w_ada, b_ada, g_ffn1, w1_gate, w1_up, w1_down, g_mix, w_in, w_dw, b_dw, ln_g, ln_b,
           w_conv_out, w_attn_o, w_out, g_ffn2, w2_gate, w2_up, w2_down, g_final):
    bsz, t, _ = x_prompt.shape
    db, s, _ = x_sample.shape
    depth = w_ada.shape[0]
    n_pool = cache_k.shape[1]
    past = page_table.shape[1] * PAGE_SIZE
    q_rows = 8
    assert depth == 1 and t >= CONV_W - 1 and s <= q_rows

    rope_p = _rope_tables(jnp.arange(t))
    rope_s = _rope_tables(jnp.tile(past + jnp.arange(s), db))
    row = lambda v: v.reshape(1, -1)

    c_all = jnp.concatenate([c_prompt, c_sample], axis=0)
    n_c = c_all.shape[0]
    c_all = jnp.pad(c_all, ((0, -n_c % 8), (0, 0)))

    hp = x_prompt
    hs = x_sample.reshape(1, db * s, D_MODEL)
    outs = [[] for _ in range(8)]
    for l in range(depth):
        m = _ada(c_all, w_ada[l], b_ada[l])
        mod_p = [m[:bsz, i * D_MODEL:(i + 1) * D_MODEL].reshape(bsz, 1, D_MODEL) for i in range(N_MOD)]
        mod_s = [jnp.repeat(m[bsz:n_c, i * D_MODEL:(i + 1) * D_MODEL], s, axis=0).reshape(1, db * s, D_MODEL)
                 for i in range(N_MOD)]
        ff1 = _ff_weights(w1_gate[l], w1_up[l], w1_down[l])
        ff2 = _ff_weights(w2_gate[l], w2_up[l], w2_down[l])
        w_ins = _in_weights(w_in[l])
        conv_w = (w_dw[l], row(b_dw[l]), row(ln_g[l]), row(ln_b[l]), w_conv_out[l].astype(BF16))
        wao, wo = w_attn_o[l].astype(BF16), w_out[l].astype(BF16)

        def front(h, mod, rope):
            h = _ffn(h, mod[0], mod[1], mod[2], row(g_ffn1[l]), *ff1)
            return h, _inproj(h, mod[3], mod[4], row(g_mix[l]), *rope, w_ins)

        def back(h, att, conv, sgc, sga, mod):
            return _out(h, att, conv, sgc, sga, mod[5], mod[6], mod[7], mod[8], row(g_ffn2[l]), row(g_final),
                        wao, wo, *ff2)

        hp, (u, q, qi, wi, sgc, sga, kt, kbt, vt, vbt, kidt, kit, knorm) = front(hp, mod_p, rope_p)
        conv = _conv_prompt(u, *conv_w)
        att = _dsa_prompt(qi, wi, q, kidt, kbt, vbt, knorm)
        hp = back(hp, att, conv, sgc, sga, mod_p)
        heads_last = lambda a: a.reshape(bsz, N_HEADS, HEAD_DIM, t).transpose(0, 3, 1, 2)
        outs[0].append(heads_last(kt))
        outs[1].append(heads_last(vt))
        outs[2].append(kit.transpose(0, 2, 1))
        outs[3].append(u[:, t - (CONV_W - 1):])

        hs, (u, q, qi, wi, sgc, sga, kt, kbt, vt, vbt, kidt, kit, _) = front(hs, mod_s, rope_s)
        per_seq = lambda a: jnp.pad(a.astype(F32).reshape(db, s, -1), ((0, 0), (0, q_rows - s), (0, 0)))

        def new_keys(a):
            f = a.shape[2]
            a = a[0].transpose(1, 0, 2).reshape(f, db, s).transpose(1, 0, 2)
            return jnp.pad(a, ((0, 0), (0, 0), (0, PAGE_SIZE - s)))

        ext_t = jnp.concatenate([state_conv[l].transpose(1, 0, 2), u.reshape(db, s, D_CONV).transpose(1, 0, 2)],
                                axis=0)
        conv = _conv_sample(ext_t, *conv_w).transpose(1, 0, 2).reshape(1, db * s, D_MODEL)
        att = _dsa_sample(page_table, per_seq(qi), per_seq(wi), per_seq(q),
                          new_keys(kidt), new_keys(kbt), new_keys(vbt),
                          cache_idx_k[l].transpose(0, 2, 1),
                          cache_k[l].transpose(0, 2, 3, 1).reshape(n_pool, ATT_DIM, PAGE_SIZE),
                          cache_v[l].transpose(0, 2, 3, 1).reshape(n_pool, ATT_DIM, PAGE_SIZE))
        att = att[:, :s].reshape(1, db * s, ATT_DIM).astype(BF16)
        hs = back(hs, att, conv, sgc, sga, mod_s)
        heads_last = lambda a: a.reshape(N_HEADS, HEAD_DIM, db, s).transpose(2, 3, 0, 1)
        outs[4].append(heads_last(kt))
        outs[5].append(heads_last(vt))
        outs[6].append(kit.reshape(IDX_DIM, db, s).transpose(1, 2, 0))
        outs[7].append(ext_t[s:].transpose(1, 0, 2))

    return (hp, hs.reshape(db, s, D_MODEL)) + tuple(jnp.stack(o) for o in outs)
```

```python
import functools

import jax
import jax.numpy as jnp
from jax import lax
from jax.experimental import pallas as pl
from jax.experimental.pallas import tpu as pltpu

F32 = jnp.float32
BF16 = jnp.bfloat16
I32 = jnp.int32

D_MODEL = 1024
N_HEADS = 8
HEAD_DIM = 64
ATT_DIM = N_HEADS * HEAD_DIM
IDX_HEADS = 8
IDX_DIM = 64
D_CONV = D_MODEL // 2
CONV_W = 31
D_FF = ((8 * D_MODEL // 3 + 255) // 256) * 256
N_MOD = 9
TOPK_MAX = 256
PAGE_SIZE = 128
ROPE_THETA = 10000.0
EPS = 1e-6
NEG = -1e30

LANES = 128
SUBLANES = 8
FF_CHUNK = 256
N_FF_CHUNKS = D_FF // FF_CHUNK
TOKEN_TILE = 512
CONV_HALO = 32
CONV_ROWS = 64
DSA_TQ = 256
DSA_TK = 512
DSA_STRIP = 64
DSA_CANDS = 12
DECODE_BITS = 4
SHIFT_LIMIT = 40.0
VMEM_LIMIT = 56 * 1024 * 1024
INT_MIN = -2 ** 31
INT_MAX = 2 ** 31 - 1
KEY_NEG_INF = INT_MIN + 2 ** 23 - 1


def _dot(a, b):
    return jnp.dot(a, b, preferred_element_type=F32)


def _dot_nt(a, b):
    return lax.dot_general(a, b, (((1,), (1,)), ((), ())), preferred_element_type=F32)


def _sigmoid(x):
    return jax.nn.sigmoid(x)


def _rms_mod(x, g, scale, shift):
    ms = jnp.mean(x * x, axis=-1, keepdims=True)
    return (x * lax.rsqrt(ms + EPS)) * g * (1.0 + scale) + shift


def _key_to_f32(key):
    return pltpu.bitcast(key ^ ((key >> 31) & INT_MAX), F32)


def _const_spec(shape):
    nd = len(shape)
    return pl.BlockSpec(shape, lambda *_: (0,) * nd, pipeline_mode=pl.Buffered(1))


def _params(n_grid):
    return pltpu.CompilerParams(dimension_semantics=("arbitrary",) * n_grid, vmem_limit_bytes=VMEM_LIMIT)


def _ada_kernel(c_ref, w_ref, b_ref, o_ref):
    c = c_ref[...]
    s = (c * _sigmoid(c)).astype(BF16)
    o_ref[...] = _dot(s, w_ref[...].astype(BF16)) + b_ref[...]


def _ada(c, w, b):
    rows = c.shape[0]
    nb = D_MODEL
    return pl.pallas_call(
        _ada_kernel,
        out_shape=jax.ShapeDtypeStruct((rows, N_MOD * D_MODEL), F32),
        grid=(N_MOD * D_MODEL // nb,),
        in_specs=[pl.BlockSpec((rows, D_MODEL), lambda j: (0, 0)),
                  pl.BlockSpec((D_MODEL, nb), lambda j: (0, j)),
                  pl.BlockSpec((1, nb), lambda j: (0, j))],
        out_specs=pl.BlockSpec((rows, nb), lambda j: (0, j)),
        compiler_params=_params(1),
        name="ada",
    )(c, w, b.reshape(1, -1))


def _swiglu_into(hn_ref, wg_ref, wu_ref, wd_ref, acc_ref):
    acc_ref[...] = jnp.zeros_like(acc_ref)

    def body(i, carry):
        hn = hn_ref[...]
        a = _dot(hn, wg_ref[i])
        b = _dot(hn, wu_ref[i])
        act = (a * _sigmoid(a) * b).astype(BF16)
        acc_ref[...] += _dot(act, wd_ref[i])
        return carry

    lax.fori_loop(0, N_FF_CHUNKS, body, 0)


def _ffn_kernel(x_ref, sh_ref, sc_ref, gt_ref, g_ref, wg_ref, wu_ref, wd_ref, o_ref, hn_ref, acc_ref):
    x = x_ref[0]
    hn_ref[...] = _rms_mod(x, g_ref[...], sc_ref[0], sh_ref[0]).astype(BF16)
    _swiglu_into(hn_ref, wg_ref, wu_ref, wd_ref, acc_ref)
    o_ref[0] = x + 0.5 * gt_ref[0] * acc_ref[...]


def _tile_spec(tm, width, per_token):
    if per_token:
        return pl.BlockSpec((1, tm, width), lambda b, i: (b, i, 0))
    return pl.BlockSpec((1, 1, width), lambda b, i: (b, 0, 0))


def _ffn(x, shift, scale, gate, g, wg, wu, wd):
    s, t, _ = x.shape
    tm = min(TOKEN_TILE, t)
    per_token = shift.shape[1] != 1
    tok = _tile_spec(tm, D_MODEL, True)
    mod = _tile_spec(tm, D_MODEL, per_token)
    return pl.pallas_call(
        _ffn_kernel,
        out_shape=jax.ShapeDtypeStruct(x.shape, F32),
        grid=(s, t // tm),
        in_specs=[tok, mod, mod, mod, _const_spec((1, D_MODEL)),
                  _const_spec(wg.shape), _const_spec(wu.shape), _const_spec(wd.shape)],
        out_specs=tok,
        scratch_shapes=[pltpu.VMEM((tm, D_MODEL), BF16), pltpu.VMEM((tm, D_MODEL), F32)],
        compiler_params=_params(2),
        name="ffn1",
    )(x, shift, scale, gate, g, wg, wu, wd)


def _rope(x, cos, sin, lo32):
    sw = jnp.where(lo32, pltpu.roll(x, 96, 1), pltpu.roll(x, 32, 1))
    return x * cos + sw * sin


def _rope_t(x, cos, sin):
    h = HEAD_DIM // 2
    sw = jnp.concatenate([x[h:2 * h], x[0:h], x[3 * h:4 * h], x[2 * h:3 * h]], axis=0)
    return x * cos + sw * sin


def _inproj_kernel(h_ref, sh_ref, sc_ref, g_ref, cos_ref, sin_ref, cost_ref, sint_ref,
                   wglu_ref, wq_ref, wqw_ref, wgt_ref, wkvt_ref,
                   u_ref, q_ref, qi_ref, wi_ref, sgc_ref, sga_ref,
                   kt_ref, kbt_ref, vt_ref, vbt_ref, kidt_ref, kit_ref, kn_ref, n_ref, *, tm, tk):
    n_ref[...] = _rms_mod(h_ref[0], g_ref[...], sc_ref[0], sh_ref[0]).astype(BF16)
    lane = lax.broadcasted_iota(I32, (1, LANES), 1)
    lo32 = (lane & (HEAD_DIM - 1)) < HEAD_DIM // 2
    rope = functools.partial(_rope, cos=cos_ref[...], sin=sin_ref[...], lo32=lo32)
    n_q = ATT_DIM // LANES

    z = _dot(n_ref[...], wglu_ref[...])
    u_ref[0] = z[:, :D_CONV] * _sigmoid(z[:, D_CONV:])

    z = _dot(n_ref[...], wq_ref[...])
    for s in range(n_q):
        sl = slice(s * LANES, (s + 1) * LANES)
        q_ref[0, :, sl] = (rope(z[:, sl]) * HEAD_DIM ** -0.5).astype(BF16)

    z = _dot(n_ref[...], wqw_ref[...])
    for s in range(n_q):
        sl = slice(s * LANES, (s + 1) * LANES)
        qi_ref[0, :, sl] = rope(z[:, sl]).astype(BF16)
    wi_ref[0] = z[:, ATT_DIM:ATT_DIM + IDX_HEADS] * (IDX_HEADS * IDX_DIM) ** -0.5

    z = _dot(n_ref[...], wgt_ref[...])
    sgc_ref[0] = _sigmoid(z[:, :D_MODEL]).astype(BF16)
    sga_ref[0] = _sigmoid(z[:, D_MODEL:]).astype(BF16)

    zt = _dot_nt(wkvt_ref[...], n_ref[...])
    cost, sint = cost_ref[...], sint_ref[...]
    chunks = [slice(j * tk, (j + 1) * tk) for j in range(tm // tk)]
    norms = []
    for s in range(n_q):
        rows = slice(s * LANES, (s + 1) * LANES)
        xk = _rope_t(zt[rows], cost, sint)
        kt_ref[0, rows, :] = xk
        xkb = xk.astype(BF16)
        for j, ch in enumerate(chunks):
            kbt_ref[0, j, rows, :] = xkb[:, ch]
        sq = xkb.astype(F32)
        sq = sq * sq
        norms += [jnp.sum(sq[:HEAD_DIM], axis=0, keepdims=True), jnp.sum(sq[HEAD_DIM:], axis=0, keepdims=True)]
    kn_ref[0] = jnp.concatenate(norms, axis=0)
    zv = zt[ATT_DIM:2 * ATT_DIM]
    vt_ref[0] = zv
    for j, ch in enumerate(chunks):
        vbt_ref[0, j] = zv[:, ch].astype(BF16)
    kid = _rope_t(zt[2 * ATT_DIM:], cost, sint)
    kit_ref[0] = kid[:IDX_DIM]
    for j, ch in enumerate(chunks):
        kidt_ref[0, j] = kid[:, ch].astype(BF16)


def _inproj(h, shift, scale, g, cos, sin, cost, sint, ws):
    s, t, _ = h.shape
    tm = min(TOKEN_TILE, t)
    tk = DSA_TK
    per_token = shift.shape[1] != 1
    mod = _tile_spec(tm, D_MODEL, per_token)
    tab = pl.BlockSpec((tm, LANES), lambda b, i: (i, 0))
    tabt = pl.BlockSpec((LANES, tm), lambda b, i: (0, i))
    tok = [(D_CONV, F32), (ATT_DIM, BF16), (IDX_HEADS * IDX_DIM, BF16), (IDX_HEADS, F32),
           (D_MODEL, BF16), (D_MODEL, BF16)]
    feat = lambda rows, dt: (jax.ShapeDtypeStruct((s, rows, t), dt),
                             pl.BlockSpec((1, rows, tm), lambda b, i: (b, 0, i)))
    chunked = lambda rows: (jax.ShapeDtypeStruct((s, t // tk, rows, tk), BF16),
                            pl.BlockSpec((1, tm // tk, rows, tk), lambda b, i: (b, i, 0, 0)))
    key_side = [feat(ATT_DIM, F32), chunked(ATT_DIM), feat(ATT_DIM, F32), chunked(ATT_DIM),
                chunked(LANES), feat(IDX_DIM, F32), feat(N_HEADS, F32)]
    return pl.pallas_call(
        functools.partial(_inproj_kernel, tm=tm, tk=tk),
        out_shape=[jax.ShapeDtypeStruct((s, t, w), dt) for w, dt in tok] + [sd for sd, _ in key_side],
        grid=(s, t // tm),
        in_specs=[_tile_spec(tm, D_MODEL, True), mod, mod, _const_spec((1, D_MODEL)), tab, tab, tabt, tabt]
                 + [_const_spec(w.shape) for w in ws],
        out_specs=[_tile_spec(tm, w, True) for w, _ in tok] + [sp for _, sp in key_side],
        scratch_shapes=[pltpu.VMEM((tm, D_MODEL), BF16)],
        compiler_params=_params(2),
        name="inproj",
    )(h, shift, scale, g, cos, sin, cost, sint, *ws)


def _ln_silu(y, g, b):
    mu = jnp.mean(y, axis=-1, keepdims=True)
    yc = y - mu
    var = jnp.mean(yc * yc, axis=-1, keepdims=True)
    y = yc * lax.rsqrt(var + EPS) * g + b
    return y * _sigmoid(y)


def _conv_kernel(u_ref, up_ref, wdw_ref, bdw_ref, lng_ref, lnb_ref, wco_ref, o_ref, ext_ref, sh_ref, y_ref, *, tm):
    i = pl.program_id(1)
    ext_ref[0:CONV_HALO, :] = jnp.where(i > 0, up_ref[0], 0.0)
    ext_ref[CONV_HALO:, :] = u_ref[0]
    n_sh = tm + CONV_HALO - SUBLANES
    for s in range(1, SUBLANES):
        sh_ref[s - 1] = ext_ref[s:s + n_sh, :]
    off = CONV_HALO - (CONV_W - 1)
    for r in range(tm // CONV_ROWS):
        acc = jnp.broadcast_to(bdw_ref[...], (CONV_ROWS, D_CONV))
        for j in range(CONV_W):
            s = (off + j) % SUBLANES
            lo = r * CONV_ROWS + off + j - s
            x = ext_ref[lo:lo + CONV_ROWS, :] if s == 0 else sh_ref[s - 1, lo:lo + CONV_ROWS, :]
            acc = acc + wdw_ref[j:j + 1, :] * x
        y_ref[r * CONV_ROWS:(r + 1) * CONV_ROWS, :] = _ln_silu(acc, lng_ref[...], lnb_ref[...]).astype(BF16)
    o_ref[0] = _dot(y_ref[...], wco_ref[...]).astype(BF16)


def _conv_prompt(u, wdw, bdw, lng, lnb, wco):
    b, t, _ = u.shape
    tm = min(TOKEN_TILE, t)
    halo_blocks = tm // CONV_HALO
    return pl.pallas_call(
        functools.partial(_conv_kernel, tm=tm),
        out_shape=jax.ShapeDtypeStruct((b, t, D_MODEL), BF16),
        grid=(b, t // tm),
        in_specs=[pl.BlockSpec((1, tm, D_CONV), lambda s, i: (s, i, 0)),
                  pl.BlockSpec((1, CONV_HALO, D_CONV), lambda s, i: (s, jnp.maximum(i * halo_blocks - 1, 0), 0)),
                  _const_spec(wdw.shape), _const_spec(bdw.shape), _const_spec(lng.shape), _const_spec(lnb.shape),
                  _const_spec(wco.shape)],
        out_specs=pl.BlockSpec((1, tm, D_MODEL), lambda s, i: (s, i, 0)),
        scratch_shapes=[pltpu.VMEM((tm + CONV_HALO, D_CONV), F32),
                        pltpu.VMEM((SUBLANES - 1, tm + CONV_HALO - SUBLANES, D_CONV), F32),
                        pltpu.VMEM((tm, D_CONV), BF16)],
        compiler_params=_params(2),
        name="conv_prompt",
    )(u, u, wdw, bdw, lng, lnb, wco)


def _conv_sample_kernel(ext_ref, wdw_ref, bdw_ref, lng_ref, lnb_ref, wco_ref, o_ref, *, steps, rows):
    for s in range(steps):
        acc = jnp.broadcast_to(bdw_ref[...], (rows, D_CONV))
        for j in range(CONV_W):
            acc = acc + wdw_ref[j:j + 1, :] * ext_ref[s + j]
        y = _ln_silu(acc, lng_ref[...], lnb_ref[...]).astype(BF16)
        o_ref[s] = _dot(y, wco_ref[...]).astype(BF16)


def _conv_sample(ext_t, wdw, bdw, lng, lnb, wco):
    n, rows, _ = ext_t.shape
    steps = n - (CONV_W - 1)
    return pl.pallas_call(
        functools.partial(_conv_sample_kernel, steps=steps, rows=rows),
        out_shape=jax.ShapeDtypeStruct((steps, rows, D_MODEL), BF16),
        grid=(1,),
        in_specs=[_const_spec(ext_t.shape), _const_spec(wdw.shape), _const_spec(bdw.shape),
                  _const_spec(lng.shape), _const_spec(lnb.shape), _const_spec(wco.shape)],
        out_specs=pl.BlockSpec((steps, rows, D_MODEL), lambda i: (0, 0, 0)),
        compiler_params=_params(1),
        name="conv_sample",
    )(ext_t, wdw, bdw, lng, lnb, wco)


def _count_term(kind, scores, kpos, thr, pos):
    if kind == "ge":
        return jnp.where(scores >= thr, 1, 0)
    if kind == "gt":
        return jnp.where(scores > thr, 1, 0)
    if kind == "eq":
        return jnp.where(scores == thr, 1, 0)
    return jnp.where(scores == thr, jnp.where(kpos < pos, 1, 0), 0)


def _select_threshold(count, shape, bits=1):
    assert 32 % bits == 0

    def pass_body(p, key):
        shift = 32 - bits * (p + 1)
        digit = jnp.zeros(shape, I32)
        for i in range(1, 2 ** bits):
            cand = key + lax.shift_left(jnp.int32(i), shift)
            digit = digit + jnp.where(count("ge", _key_to_f32(cand), None) >= TOPK_MAX, 1, 0)
        return key + lax.shift_left(digit, shift)

    key = lax.fori_loop(0, 32 // bits, pass_body, jnp.full(shape, INT_MIN, I32))
    thr = jnp.where(key < KEY_NEG_INF, -jnp.inf, _key_to_f32(key))
    need = TOPK_MAX - count("gt", thr, None)
    n_eq = count("eq", thr, None)
    return thr, need, n_eq


def _tie_position(count, thr, need, n_bits):
    def bit_body(b, j):
        cand = j + lax.shift_left(jnp.int32(1), n_bits - 1 - b)
        return jnp.where(count("tie", thr, cand) < need, cand, j)

    return lax.fori_loop(0, n_bits, bit_body, jnp.zeros(thr.shape, I32))


def _bias_from_scores(scores, kpos, qpos, thr, last_tie):
    t1 = jnp.where(kpos <= last_tie, 0.0, NEG)
    t2 = jnp.where(scores == thr, t1, NEG)
    t3 = jnp.where(scores > thr, 0.0, t2)
    return jnp.where(kpos <= qpos, t3, NEG)


def _split_heads(x, lo):
    zero = jnp.zeros_like(x)
    return jnp.where(lo, x, zero), jnp.where(lo, zero, x)


def _dsa_prompt_kernel(qi_ref, wi_ref, q_ref, kid_ref, kb_ref, vb_ref, kn_ref, o_ref,
                       qim_ref, qm_ref, wib_ref, sc_ref, s8_ref, s_ref, p_ref, cand_ref, candt_ref, drop_ref, sel_ref,
                       thr_ref, pos_ref, tie_ref, shift_ref, m_ref, l_ref, al_ref, acc_ref, *, tq, tk, n_bits):
    st = DSA_STRIP
    n_st = tq // st
    n_lt = tk // LANES
    q0 = pl.program_id(1) * tq
    n_ch = (q0 + tq + tk - 1) // tk
    lo = lax.broadcasted_iota(I32, (tq, LANES), 1) < HEAD_DIM
    strips = [slice(r * st, (r + 1) * st) for r in range(n_st)]
    tiles = [slice(t * LANES, (t + 1) * LANES) for t in range(n_lt)]
    lane_s = lax.broadcasted_iota(I32, (st, LANES), 1)
    row_s = lax.broadcasted_iota(I32, (st, LANES), 0)

    for pr in range(N_HEADS // 2):
        sl = slice(pr * LANES, (pr + 1) * LANES)
        qim_ref[2 * pr], qim_ref[2 * pr + 1] = _split_heads(qi_ref[0, :, sl], lo)
        qm_ref[pr, 0:tq], qm_ref[pr, tq:2 * tq] = _split_heads(q_ref[0, :, sl], lo)
    for h in range(IDX_HEADS):
        wib_ref[h] = jnp.broadcast_to(wi_ref[0, :, h:h + 1], (tq, tk))
    cand_ref[...] = jnp.full(cand_ref.shape, -jnp.inf, F32)
    drop_ref[...] = jnp.full(drop_ref.shape, -jnp.inf, F32)

    def score_chunk(c, carry):
        kc = kid_ref[0, c]
        for h in range(IDX_HEADS):
            s8_ref[h] = _dot(qim_ref[h], kc)
        for r, rows in enumerate(strips):
            acc = jnp.zeros((st, tk), F32)
            for h in range(IDX_HEADS):
                acc = acc + wib_ref[h, rows, :] * jnp.maximum(s8_ref[h, rows, :], 0.0)
            xs = []
            for t, tl in enumerate(tiles):
                visible = lane_s + (c * tk + t * LANES) <= row_s + (q0 + r * st)
                xs.append(jnp.where(visible, acc[:, tl], NEG))
                sc_ref[c, rows, tl] = xs[t]
            for j in range(DSA_CANDS):
                kept = cand_ref[j, rows, :]
                for t in range(n_lt):
                    kept, xs[t] = jnp.maximum(kept, xs[t]), jnp.minimum(kept, xs[t])
                cand_ref[j, rows, :] = kept
            dropped = drop_ref[rows, :]
            for x in xs:
                dropped = jnp.maximum(dropped, x)
            drop_ref[rows, :] = dropped
        return carry

    lax.fori_loop(0, n_ch, score_chunk, 0)

    def count(kind, thr, pos):
        thr_ref[...] = jnp.broadcast_to(thr, (tq, LANES))
        if pos is not None:
            pos_ref[...] = jnp.broadcast_to(pos, (tq, LANES))

        def body(c, accs):
            out = []
            for r, rows in enumerate(strips):
                thr_b = thr_ref[rows, :]
                pos_b = pos_ref[rows, :] if pos is not None else None
                a = accs[r]
                for t, tl in enumerate(tiles):
                    kpos = lane_s + (c * tk + t * LANES)
                    a = a + _count_term(kind, sc_ref[c, rows, tl], kpos, thr_b, pos_b)
                out.append(a)
            return tuple(out)

        accs = lax.fori_loop(0, n_ch, body, tuple(jnp.zeros((st, LANES), I32) for _ in strips))
        return jnp.concatenate([jnp.sum(a.astype(F32), axis=1, keepdims=True) for a in accs], axis=0)

    for j in range(DSA_CANDS):
        candt_ref[j * LANES:(j + 1) * LANES, :] = cand_ref[j].T

    def count_kept(kind, thr, pos):
        thr_b = jnp.broadcast_to(thr, (8, tq))
        accs = [jnp.zeros((8, tq), I32) for _ in range(4)]
        for g in range(DSA_CANDS * LANES // 8):
            term = _count_term(kind, candt_ref[g * 8:(g + 1) * 8, :], None, thr_b, None)
            accs[g % 4] = accs[g % 4] + term
        total = (accs[0] + accs[1]) + (accs[2] + accs[3])
        return jnp.sum(total.astype(F32), axis=0, keepdims=True)

    def to_cols(x):
        eye = lax.broadcasted_iota(I32, (LANES, LANES), 0) == lax.broadcasted_iota(I32, (LANES, LANES), 1)
        blocks = []
        for b in range(tq // LANES):
            xb = jnp.broadcast_to(x[:, b * LANES:(b + 1) * LANES], (LANES, LANES))
            col = jnp.sum(jnp.where(eye, xb, 0.0), axis=1, keepdims=True)
            blocks.append(jnp.broadcast_to(col, (LANES, LANES)))
        return jnp.concatenate(blocks, axis=0)

    for v, x in enumerate(_select_threshold(count_kept, (1, tq))):
        sel_ref[v] = to_cols(x)
    dropped = drop_ref[...]
    unsafe = jnp.where(dropped > -jnp.inf, jnp.where(dropped >= sel_ref[0], 1.0, 0.0), 0.0)

    @pl.when(jnp.max(unsafe) > 0.5)
    def _():
        for v, x in enumerate(_select_threshold(count, (tq, 1))):
            sel_ref[v] = jnp.broadcast_to(x, (tq, LANES))

    thr, need, n_eq = (sel_ref[v, :, 0:1] for v in range(3))
    tie_ref[...] = jnp.full((tq, LANES), INT_MAX, I32)
    split = n_eq > need

    @pl.when(jnp.max(jnp.where(split, 1.0, 0.0)) > 0.5)
    def _():
        j = _tie_position(count, thr, need, n_bits)
        tie_ref[...] = jnp.broadcast_to(jnp.where(split, j, INT_MAX), (tq, LANES))

    thr_ref[...] = jnp.broadcast_to(thr, (tq, LANES))

    def to_bias(c, carry):
        for r, rows in enumerate(strips):
            thr_b, tie_b = thr_ref[rows, :], tie_ref[rows, :]
            qpos = row_s + (q0 + r * st)
            for t, tl in enumerate(tiles):
                kpos = lane_s + (c * tk + t * LANES)
                sc_ref[c, rows, tl] = _bias_from_scores(sc_ref[c, rows, tl], kpos, qpos, thr_b, tie_b)
        return carry

    lax.fori_loop(0, n_ch, to_bias, 0)

    l_ref[...] = jnp.zeros(l_ref.shape, F32)
    acc_ref[...] = jnp.zeros(acc_ref.shape, F32)

    kn = kn_ref[0]
    visible = lax.broadcasted_iota(I32, kn.shape, 1) < q0 + tq
    kmax = jnp.sqrt(jnp.max(jnp.where(visible, kn, 0.0), axis=1, keepdims=True))
    for h in range(N_HEADS):
        qh = qm_ref[h // 2, (h % 2) * tq:(h % 2 + 1) * tq, :].astype(F32)
        qn = jnp.sqrt(jnp.sum(qh * qh, axis=1, keepdims=True))
        shift_ref[h] = jnp.broadcast_to(qn * kmax[h:h + 1, :], (tq, LANES))
    bounded = jnp.max(shift_ref[...]) <= SHIFT_LIMIT

    def attend_chunk_bounded(c, carry):
        for pr in range(N_HEADS // 2):
            feat = slice(pr * LANES, (pr + 1) * LANES)
            s_ref[pr] = _dot(qm_ref[pr], kb_ref[0, c, feat, :])
            for e in range(2):
                h = 2 * pr + e
                for r, rows in enumerate(strips):
                    srows = slice(e * tq + r * st, e * tq + (r + 1) * st)
                    shift = shift_ref[h, rows, :]
                    psum = l_ref[h, rows, :]
                    for tl in tiles:
                        p = jnp.exp(s_ref[pr, srows, tl] + sc_ref[c, rows, tl] - shift)
                        psum = psum + p
                        p_ref[pr, srows, tl] = p.astype(BF16)
                    l_ref[h, rows, :] = psum
            pv = _dot_nt(p_ref[pr], vb_ref[0, c, feat, :])
            acc_ref[pr] = acc_ref[pr] + jnp.where(lo, pv[:tq], pv[tq:])
        return carry

    @pl.when(bounded)
    def _():
        lax.fori_loop(0, n_ch, attend_chunk_bounded, 0)

    def attend_chunk(c, carry):
        for pr in range(N_HEADS // 2):
            feat = slice(pr * LANES, (pr + 1) * LANES)
            s_ref[pr] = _dot(qm_ref[pr], kb_ref[0, c, feat, :])
            for e in range(2):
                h = 2 * pr + e
                for r, rows in enumerate(strips):
                    srows = slice(e * tq + r * st, e * tq + (r + 1) * st)
                    s = [s_ref[pr, srows, tl] + sc_ref[c, rows, tl] for tl in tiles]
                    mx = s[0]
                    for x in s[1:]:
                        mx = jnp.maximum(mx, x)
                    m_prev = m_ref[h, rows, :]
                    m_new = jnp.maximum(m_prev, jnp.max(mx, axis=1, keepdims=True))
                    alpha = jnp.exp(m_prev - m_new)
                    p = [jnp.exp(x - m_new) for x in s]
                    psum = p[0]
                    for x in p[1:]:
                        psum = psum + x
                    l_ref[h, rows, :] = alpha * l_ref[h, rows, :] + psum
                    m_ref[h, rows, :] = m_new
                    al_ref[e, rows, :] = alpha
                    for x, tl in zip(p, tiles):
                        p_ref[pr, srows, tl] = x.astype(BF16)
            pv = _dot_nt(p_ref[pr], vb_ref[0, c, feat, :])
            a = acc_ref[pr]
            acc_ref[pr] = jnp.where(lo, a * al_ref[0] + pv[:tq], a * al_ref[1] + pv[tq:])
        return carry

    @pl.when(jnp.logical_not(bounded))
    def _():
        m_ref[...] = jnp.full(m_ref.shape, NEG, F32)
        lax.fori_loop(0, n_ch, attend_chunk, 0)

    for pr in range(N_HEADS // 2):
        l_e = jnp.sum(l_ref[2 * pr], axis=1, keepdims=True)
        l_o = jnp.sum(l_ref[2 * pr + 1], axis=1, keepdims=True)
        o_ref[0, :, pr * LANES:(pr + 1) * LANES] = (acc_ref[pr] * jnp.where(lo, 1.0 / l_e, 1.0 / l_o)).astype(BF16)


def _dsa_prompt(qi, wi, q, kidt, kbt, vbt, knorm):
    b, t, _ = q.shape
    tq, tk = DSA_TQ, DSA_TK
    assert tk % tq == 0 and t % tk == 0
    n_bits = max(1, (t - 1).bit_length())
    qspec = lambda w: pl.BlockSpec((1, tq, w), lambda s, i: (s, i, 0))
    kspec = lambda rows: pl.BlockSpec((1, t // tk, rows, tk), lambda s, i: (s, 0, 0, 0),
                                      pipeline_mode=pl.Buffered(1))
    return pl.pallas_call(
        functools.partial(_dsa_prompt_kernel, tq=tq, tk=tk, n_bits=n_bits),
        out_shape=jax.ShapeDtypeStruct((b, t, ATT_DIM), BF16),
        grid=(b, t // tq),
        in_specs=[qspec(IDX_HEADS * IDX_DIM), qspec(IDX_HEADS), qspec(ATT_DIM),
                  kspec(LANES), kspec(ATT_DIM), kspec(ATT_DIM),
                  pl.BlockSpec((1, N_HEADS, t), lambda s, i: (s, 0, 0), pipeline_mode=pl.Buffered(1))],
        out_specs=qspec(ATT_DIM),
        scratch_shapes=[pltpu.VMEM((IDX_HEADS, tq, LANES), BF16),
                        pltpu.VMEM((N_HEADS // 2, 2 * tq, LANES), BF16),
                        pltpu.VMEM((IDX_HEADS, tq, tk), F32),
                        pltpu.VMEM((t // tk, tq, tk), F32),
                        pltpu.VMEM((IDX_HEADS, tq, tk), F32),
                        pltpu.VMEM((N_HEADS // 2, 2 * tq, tk), F32),
                        pltpu.VMEM((N_HEADS // 2, 2 * tq, tk), BF16),
                        pltpu.VMEM((DSA_CANDS, tq, LANES), F32),
                        pltpu.VMEM((DSA_CANDS * LANES, tq), F32),
                        pltpu.VMEM((tq, LANES), F32),
                        pltpu.VMEM((3, tq, LANES), F32),
                        pltpu.VMEM((tq, LANES), F32),
                        pltpu.VMEM((tq, LANES), I32),
                        pltpu.VMEM((tq, LANES), I32),
                        pltpu.VMEM((N_HEADS, tq, LANES), F32),
                        pltpu.VMEM((N_HEADS, tq, LANES), F32),
                        pltpu.VMEM((N_HEADS, tq, LANES), F32),
                        pltpu.VMEM((2, tq, LANES), F32),
                        pltpu.VMEM((N_HEADS // 2, tq, LANES), F32)],
        compiler_params=_params(2),
        name="dsa_prompt",
    )(qi, wi, q, kidt, kbt, vbt, knorm)


def _dsa_sample_kernel(pt_ref, qi_ref, wi_ref, q_ref, kidn_ref, kn_ref, vn_ref, *rest, n_pages, rows, past):
    idx_refs = rest[:n_pages]
    k_refs = rest[n_pages:2 * n_pages]
    v_refs = rest[2 * n_pages:3 * n_pages]
    o_ref, kid_all, k_all, v_all, tie_ref = rest[3 * n_pages:]
    n_keys = past + PAGE_SIZE

    for p in range(n_pages):
        sl = slice(p * PAGE_SIZE, (p + 1) * PAGE_SIZE)
        x = idx_refs[p][0]
        kid_all[:, sl] = jnp.concatenate([x, x], axis=0).astype(BF16)
        k_all[:, sl] = k_refs[p][0].astype(BF16)
        v_all[:, sl] = v_refs[p][0].astype(BF16)
    tail = slice(past, n_keys)
    kid_all[:, tail] = kidn_ref[0]
    k_all[:, tail] = kn_ref[0]
    v_all[:, tail] = vn_ref[0]

    lo = lax.broadcasted_iota(I32, (rows, LANES), 1) < HEAD_DIM
    qi = qi_ref[0]
    parts = []
    for pr in range(IDX_HEADS // 2):
        parts.extend(_split_heads(qi[:, pr * LANES:(pr + 1) * LANES], lo))
    s_all = _dot(jnp.concatenate(parts, axis=0).astype(BF16), kid_all[...])
    wi = wi_ref[0]
    acc = jnp.zeros((rows, n_keys), F32)
    for h in range(IDX_HEADS):
        acc = acc + wi[:, h:h + 1] * jnp.maximum(s_all[h * rows:(h + 1) * rows], 0.0)
    kpos = lax.broadcasted_iota(I32, (rows, n_keys), 1)
    qpos = lax.broadcasted_iota(I32, (rows, n_keys), 0) + past
    scores = jnp.where(kpos <= qpos, acc, NEG)

    def count(kind, thr, pos):
        return jnp.sum(_count_term(kind, scores, kpos, thr, pos).astype(F32), axis=1, keepdims=True)

    n_bits = (n_keys - 1).bit_length()
    thr, need, n_eq = _select_threshold(count, (rows, 1), DECODE_BITS)
    tie_ref[...] = jnp.full((rows, 1), INT_MAX, I32)
    split = n_eq > need

    @pl.when(jnp.max(jnp.where(split, 1.0, 0.0)) > 0.5)
    def _():
        tie_ref[...] = jnp.where(split, _tie_position(count, thr, need, n_bits), INT_MAX)

    bias = _bias_from_scores(scores, kpos, qpos, thr, tie_ref[...])
    bias2 = jnp.concatenate([bias, bias], axis=0)

    q = q_ref[0]
    for pr in range(N_HEADS // 2):
        sl = slice(pr * LANES, (pr + 1) * LANES)
        lhs = jnp.concatenate(_split_heads(q[:, sl], lo), axis=0).astype(BF16)
        s = _dot(lhs, k_all[sl, :]) + bias2
        p = jnp.exp(s - jnp.max(s, axis=1, keepdims=True))
        l = jnp.sum(p, axis=1, keepdims=True)
        pv = _dot_nt(p.astype(BF16), v_all[sl, :]) / l
        o_ref[0, :, sl] = jnp.where(lo, pv[:rows], pv[rows:])


def _dsa_sample(page_table, qi, wi, q, kidn, kn, vn, cache_idx, cache_k, cache_v):
    db, rows, _ = q.shape
    n_pages = page_table.shape[1]
    past = n_pages * PAGE_SIZE
    n_keys = past + PAGE_SIZE
    seq = lambda w: pl.BlockSpec((1, rows, w), lambda s, pt: (s, 0, 0))
    new = lambda f: pl.BlockSpec((1, f, PAGE_SIZE), lambda s, pt: (s, 0, 0))

    def page(f, p):
        return pl.BlockSpec((1, f, PAGE_SIZE), lambda s, pt: (pt[s * n_pages + p], 0, 0))

    grid_spec = pltpu.PrefetchScalarGridSpec(
        num_scalar_prefetch=1,
        grid=(db,),
        in_specs=[seq(IDX_HEADS * IDX_DIM), seq(IDX_HEADS), seq(ATT_DIM), new(LANES), new(ATT_DIM), new(ATT_DIM)]
                 + [page(IDX_DIM, p) for p in range(n_pages)]
                 + [page(ATT_DIM, p) for p in range(n_pages)]
                 + [page(ATT_DIM, p) for p in range(n_pages)],
        out_specs=seq(ATT_DIM),
        scratch_shapes=[pltpu.VMEM((LANES, n_keys), BF16), pltpu.VMEM((ATT_DIM, n_keys), BF16),
                        pltpu.VMEM((ATT_DIM, n_keys), BF16), pltpu.VMEM((rows, 1), I32)],
    )
    return pl.pallas_call(
        functools.partial(_dsa_sample_kernel, n_pages=n_pages, rows=rows, past=past),
        out_shape=jax.ShapeDtypeStruct((db, rows, ATT_DIM), F32),
        grid_spec=grid_spec,
        compiler_params=_params(1),
        name="dsa_sample",
    )(page_table.reshape(-1), qi, wi, q, kidn, kn, vn,
      *([cache_idx] * n_pages), *([cache_k] * n_pages), *([cache_v] * n_pages))


def _out_kernel(h_ref, att_ref, conv_ref, sgc_ref, sga_ref, g2_ref, sh_ref, sc_ref, g3_ref, gn_ref, gfin_ref,
                wao_ref, wo_ref, wg_ref, wu_ref, wd_ref, y_ref, hn_ref, acc_ref):
    att_o = _dot(att_ref[0], wao_ref[...])
    mix = sgc_ref[0].astype(F32) * conv_ref[0].astype(F32) + sga_ref[0].astype(F32) * att_o
    h2 = h_ref[0] + g2_ref[0] * _dot(mix.astype(BF16), wo_ref[...])
    y_ref[0] = h2
    hn_ref[...] = _rms_mod(h2, gn_ref[...], sc_ref[0], sh_ref[0]).astype(BF16)
    _swiglu_into(hn_ref, wg_ref, wu_ref, wd_ref, acc_ref)
    h3 = y_ref[0] + 0.5 * g3_ref[0] * acc_ref[...]
    ms = jnp.mean(h3 * h3, axis=-1, keepdims=True)
    y_ref[0] = h3 * lax.rsqrt(ms + EPS) * gfin_ref[...]


def _out(h, att, conv, sgc, sga, g2, shift, scale, g3, gn, gfin, wao, wo, wg, wu, wd):
    s, t, _ = h.shape
    tm = min(TOKEN_TILE, t)
    per_token = shift.shape[1] != 1
    mod = _tile_spec(tm, D_MODEL, per_token)
    tok = lambda w: _tile_spec(tm, w, True)
    return pl.pallas_call(
        _out_kernel,
        out_shape=jax.ShapeDtypeStruct(h.shape, F32),
        grid=(s, t // tm),
        in_specs=[tok(D_MODEL), tok(ATT_DIM), tok(D_MODEL), tok(D_MODEL), tok(D_MODEL), mod, mod, mod, mod,
                  _const_spec((1, D_MODEL)), _const_spec((1, D_MODEL)),
                  _const_spec(wao.shape), _const_spec(wo.shape),
                  _const_spec(wg.shape), _const_spec(wu.shape), _const_spec(wd.shape)],
        out_specs=tok(D_MODEL),
        scratch_shapes=[pltpu.VMEM((tm, D_MODEL), BF16), pltpu.VMEM((tm, D_MODEL), F32)],
        compiler_params=_params(2),
        name="out_ffn2",
    )(h, att, conv, sgc, sga, g2, shift, scale, g3, gn, gfin, wao, wo, wg, wu, wd)


def _rope_tables(pos):
    inv = ROPE_THETA ** (-jnp.arange(0, HEAD_DIM, 2, dtype=F32) / HEAD_DIM)
    ang = pos.astype(F32)[:, None] * inv[None, :]
    cos, sin = jnp.cos(ang), jnp.sin(ang)
    cos = jnp.concatenate([cos] * 4, axis=1)
    sin = jnp.concatenate([-sin, sin, -sin, sin], axis=1)
    return cos, sin, cos.T, sin.T


def _ff_weights(wg, wu, wd):
    split_cols = lambda w: w.reshape(D_MODEL, N_FF_CHUNKS, FF_CHUNK).transpose(1, 0, 2).astype(BF16)
    return split_cols(wg), split_cols(wu), wd.reshape(N_FF_CHUNKS, FF_CHUNK, D_MODEL).astype(BF16)


def _in_weights(w_in):
    cuts = [0]
    for width in (D_CONV, D_CONV, ATT_DIM, ATT_DIM, ATT_DIM, IDX_HEADS * IDX_DIM, IDX_DIM, IDX_HEADS,
                  D_MODEL, D_MODEL):
        cuts.append(cuts[-1] + width)
    piece = lambda a, b: w_in[:, cuts[a]:cuts[b]]
    w_qw = jnp.concatenate([piece(5, 6), piece(7, 8), jnp.zeros((D_MODEL, LANES - IDX_HEADS), F32)], axis=1)
    w_ki = piece(6, 7)
    w_kvt = jnp.concatenate([piece(3, 5), w_ki, w_ki], axis=1).T
    return [w.astype(BF16) for w in (piece(0, 2), piece(2, 3), w_qw, piece(8, 10), w_kvt)]


def kernel(x_prompt, x_sample, cache_k, cache_v, cache_idx_k, state_conv, page_table, c_prompt, c_sample,
           w_ada, b_ada, g_ffn1, w1_gate, w1_up, w1_down, g_mix, w_in, w_dw, b_dw, ln_g, ln_b,
           w_conv_out, w_attn_o, w_out, g_ffn2, w2_gate, w2_up, w2_down, g_final):
    bsz, t, _ = x_prompt.shape
    db, s, _ = x_sample.shape
    depth = w_ada.shape[0]
    n_pool = cache_k.shape[1]
    past = page_table.shape[1] * PAGE_SIZE
    q_rows = 8
    assert depth == 1 and t >= CONV_W - 1 and s <= q_rows

    rope_p = _rope_tables(jnp.arange(t))
    rope_s = _rope_tables(jnp.tile(past + jnp.arange(s), db))
    row = lambda v: v.reshape(1, -1)

    c_all = jnp.concatenate([c_prompt, c_sample], axis=0)
    n_c = c_all.shape[0]
    c_all = jnp.pad(c_all, ((0, -n_c % 8), (0, 0)))

    hp = x_prompt
    hs = x_sample.reshape(1, db * s, D_MODEL)
    outs = [[] for _ in range(8)]
    for l in range(depth):
        m = _ada(c_all, w_ada[l], b_ada[l])
        mod_p = [m[:bsz, i * D_MODEL:(i + 1) * D_MODEL].reshape(bsz, 1, D_MODEL) for i in range(N_MOD)]
        mod_s = [jnp.repeat(m[bsz:n_c, i * D_MODEL:(i + 1) * D_MODEL], s, axis=0).reshape(1, db * s, D_MODEL)
                 for i in range(N_MOD)]
        ff1 = _ff_weights(w1_gate[l], w1_up[l], w1_down[l])
        ff2 = _ff_weights(w2_gate[l], w2_up[l], w2_down[l])
        w_ins = _in_weights(w_in[l])
        conv_w = (w_dw[l], row(b_dw[l]), row(ln_g[l]), row(ln_b[l]), w_conv_out[l].astype(BF16))
        wao, wo = w_attn_o[l].astype(BF16), w_out[l].astype(BF16)

        def front(h, mod, rope):
            h = _ffn(h, mod[0], mod[1], mod[2], row(g_ffn1[l]), *ff1)
            return h, _inproj(h, mod[3], mod[4], row(g_mix[l]), *rope, w_ins)

        def back(h, att, conv, sgc, sga, mod):
            return _out(h, att, conv, sgc, sga, mod[5], mod[6], mod[7], mod[8], row(g_ffn2[l]), row(g_final),
                        wao, wo, *ff2)

        hp, (u, q, qi, wi, sgc, sga, kt, kbt, vt, vbt, kidt, kit, knorm) = front(hp, mod_p, rope_p)
        conv = _conv_prompt(u, *conv_w)
        att = _dsa_prompt(qi, wi, q, kidt, kbt, vbt, knorm)
        hp = back(hp, att, conv, sgc, sga, mod_p)
        heads_last = lambda a: a.reshape(bsz, N_HEADS, HEAD_DIM, t).transpose(0, 3, 1, 2)
        outs[0].append(heads_last(kt))
        outs[1].append(heads_last(vt))
        outs[2].append(kit.transpose(0, 2, 1))
        outs[3].append(u[:, t - (CONV_W - 1):])

        hs, (u, q, qi, wi, sgc, sga, kt, kbt, vt, vbt, kidt, kit, _) = front(hs, mod_s, rope_s)
        per_seq = lambda a: jnp.pad(a.astype(F32).reshape(db, s, -1), ((0, 0), (0, q_rows - s), (0, 0)))

        def new_keys(a):
            f = a.shape[2]
            a = a[0].transpose(1, 0, 2).reshape(f, db, s).transpose(1, 0, 2)
            return jnp.pad(a, ((0, 0), (0, 0), (0, PAGE_SIZE - s)))

        ext_t = jnp.concatenate([state_conv[l].transpose(1, 0, 2), u.reshape(db, s, D_CONV).transpose(1, 0, 2)],
                                axis=0)
        conv = _conv_sample(ext_t, *conv_w).transpose(1, 0, 2).reshape(1, db * s, D_MODEL)
        att = _dsa_sample(page_table, per_seq(qi), per_seq(wi), per_seq(q),
                          new_keys(kidt), new_keys(kbt), new_keys(vbt),
                          cache_idx_k[l].transpose(0, 2, 1),
                          cache_k[l].transpose(0, 2, 3, 1).reshape(n_pool, ATT_DIM, PAGE_SIZE),
                          cache_v[l].transpose(0, 2, 3, 1).reshape(n_pool, ATT_DIM, PAGE_SIZE))
        att = att[:, :s].reshape(1, db * s, ATT_DIM).astype(BF16)
        hs = back(hs, att, conv, sgc, sga, mod_s)
        heads_last = lambda a: a.reshape(N_HEADS, HEAD_DIM, db, s).transpose(2, 3, 0, 1)
        outs[4].append(heads_last(kt))
        outs[5].append(heads_last(vt))
        outs[6].append(kit.reshape(IDX_DIM, db, s).transpose(1, 2, 0))
        outs[7].append(ext_t[s:].transpose(1, 0, 2))

    return (hp, hs.reshape(db, s, D_MODEL)) + tuple(jnp.stack(o) for o in outs)
```

```python
import functools

import jax
import jax.numpy as jnp
from jax import lax
from jax.experimental import pallas as pl
from jax.experimental.pallas import tpu as pltpu

F32 = jnp.float32
BF16 = jnp.bfloat16
I32 = jnp.int32

D_MODEL = 1024
N_HEADS = 8
HEAD_DIM = 64
ATT_DIM = N_HEADS * HEAD_DIM
IDX_HEADS = 8
IDX_DIM = 64
D_CONV = D_MODEL // 2
CONV_W = 31
D_FF = ((8 * D_MODEL // 3 + 255) // 256) * 256
N_MOD = 9
TOPK_MAX = 256
PAGE_SIZE = 128
ROPE_THETA = 10000.0
EPS = 1e-6
NEG = -1e30

LANES = 128
SUBLANES = 8
FF_CHUNK = 256
N_FF_CHUNKS = D_FF // FF_CHUNK
TOKEN_TILE = 512
CONV_HALO = 32
CONV_ROWS = 64
DSA_TQ = 256
DSA_TK = 512
DSA_STRIP = 64
DSA_CANDS = 12
DECODE_BITS = 4
DECODE_GROUP = 2
SHIFT_LIMIT = 40.0
VMEM_LIMIT = 56 * 1024 * 1024
INT_MIN = -2 ** 31
INT_MAX = 2 ** 31 - 1
KEY_NEG_INF = INT_MIN + 2 ** 23 - 1


def _dot(a, b):
    return jnp.dot(a, b, preferred_element_type=F32)


def _dot_nt(a, b):
    return lax.dot_general(a, b, (((1,), (1,)), ((), ())), preferred_element_type=F32)


def _sigmoid(x):
    return jax.nn.sigmoid(x)


def _rms_mod(x, g, scale, shift):
    ms = jnp.mean(x * x, axis=-1, keepdims=True)
    return (x * lax.rsqrt(ms + EPS)) * g * (1.0 + scale) + shift


def _key_to_f32(key):
    return pltpu.bitcast(key ^ ((key >> 31) & INT_MAX), F32)


def _const_spec(shape):
    nd = len(shape)
    return pl.BlockSpec(shape, lambda *_: (0,) * nd, pipeline_mode=pl.Buffered(1))


def _params(n_grid):
    return pltpu.CompilerParams(dimension_semantics=("arbitrary",) * n_grid, vmem_limit_bytes=VMEM_LIMIT)


def _ada_kernel(c_ref, w_ref, b_ref, o_ref):
    c = c_ref[...]
    s = (c * _sigmoid(c)).astype(BF16)
    o_ref[...] = _dot(s, w_ref[...].astype(BF16)) + b_ref[...]


def _ada(c, w, b):
    rows = c.shape[0]
    nb = D_MODEL
    return pl.pallas_call(
        _ada_kernel,
        out_shape=jax.ShapeDtypeStruct((rows, N_MOD * D_MODEL), F32),
        grid=(N_MOD * D_MODEL // nb,),
        in_specs=[pl.BlockSpec((rows, D_MODEL), lambda j: (0, 0)),
                  pl.BlockSpec((D_MODEL, nb), lambda j: (0, j)),
                  pl.BlockSpec((1, nb), lambda j: (0, j))],
        out_specs=pl.BlockSpec((rows, nb), lambda j: (0, j)),
        compiler_params=_params(1),
        name="ada",
    )(c, w, b.reshape(1, -1))


def _swiglu_into(hn_ref, wg_ref, wu_ref, wd_ref, acc_ref):
    acc_ref[...] = jnp.zeros_like(acc_ref)

    def body(i, carry):
        hn = hn_ref[...]
        a = _dot(hn, wg_ref[i])
        b = _dot(hn, wu_ref[i])
        act = (a * _sigmoid(a) * b).astype(BF16)
        acc_ref[...] += _dot(act, wd_ref[i])
        return carry

    lax.fori_loop(0, N_FF_CHUNKS, body, 0)


def _ffn_kernel(x_ref, sh_ref, sc_ref, gt_ref, g_ref, wg_ref, wu_ref, wd_ref, o_ref, hn_ref, acc_ref):
    x = x_ref[0]
    hn_ref[...] = _rms_mod(x, g_ref[...], sc_ref[0], sh_ref[0]).astype(BF16)
    _swiglu_into(hn_ref, wg_ref, wu_ref, wd_ref, acc_ref)
    o_ref[0] = x + 0.5 * gt_ref[0] * acc_ref[...]


def _tile_spec(tm, width, per_token):
    if per_token:
        return pl.BlockSpec((1, tm, width), lambda b, i: (b, i, 0))
    return pl.BlockSpec((1, 1, width), lambda b, i: (b, 0, 0))


def _ffn(x, shift, scale, gate, g, wg, wu, wd):
    s, t, _ = x.shape
    tm = min(TOKEN_TILE, t)
    per_token = shift.shape[1] != 1
    tok = _tile_spec(tm, D_MODEL, True)
    mod = _tile_spec(tm, D_MODEL, per_token)
    return pl.pallas_call(
        _ffn_kernel,
        out_shape=jax.ShapeDtypeStruct(x.shape, F32),
        grid=(s, t // tm),
        in_specs=[tok, mod, mod, mod, _const_spec((1, D_MODEL)),
                  _const_spec(wg.shape), _const_spec(wu.shape), _const_spec(wd.shape)],
        out_specs=tok,
        scratch_shapes=[pltpu.VMEM((tm, D_MODEL), BF16), pltpu.VMEM((tm, D_MODEL), F32)],
        compiler_params=_params(2),
        name="ffn1",
    )(x, shift, scale, gate, g, wg, wu, wd)


def _rope(x, cos, sin, lo32):
    sw = jnp.where(lo32, pltpu.roll(x, 96, 1), pltpu.roll(x, 32, 1))
    return x * cos + sw * sin


def _rope_t(x, cos, sin):
    h = HEAD_DIM // 2
    sw = jnp.concatenate([x[h:2 * h], x[0:h], x[3 * h:4 * h], x[2 * h:3 * h]], axis=0)
    return x * cos + sw * sin


def _inproj_kernel(h_ref, sh_ref, sc_ref, g_ref, cos_ref, sin_ref, cost_ref, sint_ref,
                   wglu_ref, wq_ref, wqw_ref, wgt_ref, wkvt_ref,
                   u_ref, q_ref, qi_ref, wi_ref, sgc_ref, sga_ref,
                   kt_ref, kbt_ref, vt_ref, vbt_ref, kidt_ref, kit_ref, kn_ref, n_ref, *, tm, tk):
    n_ref[...] = _rms_mod(h_ref[0], g_ref[...], sc_ref[0], sh_ref[0]).astype(BF16)
    lane = lax.broadcasted_iota(I32, (1, LANES), 1)
    lo32 = (lane & (HEAD_DIM - 1)) < HEAD_DIM // 2
    rope = functools.partial(_rope, cos=cos_ref[...], sin=sin_ref[...], lo32=lo32)
    n_q = ATT_DIM // LANES

    z = _dot(n_ref[...], wglu_ref[...])
    u_ref[0] = z[:, :D_CONV] * _sigmoid(z[:, D_CONV:])

    z = _dot(n_ref[...], wq_ref[...])
    for s in range(n_q):
        sl = slice(s * LANES, (s + 1) * LANES)
        q_ref[0, :, sl] = (rope(z[:, sl]) * HEAD_DIM ** -0.5).astype(BF16)

    z = _dot(n_ref[...], wqw_ref[...])
    for s in range(n_q):
        sl = slice(s * LANES, (s + 1) * LANES)
        qi_ref[0, :, sl] = rope(z[:, sl]).astype(BF16)
    wi_ref[0] = z[:, ATT_DIM:ATT_DIM + IDX_HEADS] * (IDX_HEADS * IDX_DIM) ** -0.5

    z = _dot(n_ref[...], wgt_ref[...])
    sgc_ref[0] = _sigmoid(z[:, :D_MODEL]).astype(BF16)
    sga_ref[0] = _sigmoid(z[:, D_MODEL:]).astype(BF16)

    zt = _dot_nt(wkvt_ref[...], n_ref[...])
    cost, sint = cost_ref[...], sint_ref[...]
    chunks = [slice(j * tk, (j + 1) * tk) for j in range(tm // tk)]
    norms = []
    for s in range(n_q):
        rows = slice(s * LANES, (s + 1) * LANES)
        xk = _rope_t(zt[rows], cost, sint)
        kt_ref[0, rows, :] = xk
        xkb = xk.astype(BF16)
        for j, ch in enumerate(chunks):
            kbt_ref[0, j, rows, :] = xkb[:, ch]
        sq = xkb.astype(F32)
        sq = sq * sq
        norms += [jnp.sum(sq[:HEAD_DIM], axis=0, keepdims=True), jnp.sum(sq[HEAD_DIM:], axis=0, keepdims=True)]
    kn_ref[0] = jnp.concatenate(norms, axis=0)
    zv = zt[ATT_DIM:2 * ATT_DIM]
    vt_ref[0] = zv
    for j, ch in enumerate(chunks):
        vbt_ref[0, j] = zv[:, ch].astype(BF16)
    kid = _rope_t(zt[2 * ATT_DIM:], cost, sint)
    kit_ref[0] = kid[:IDX_DIM]
    for j, ch in enumerate(chunks):
        kidt_ref[0, j] = kid[:, ch].astype(BF16)


def _inproj(h, shift, scale, g, cos, sin, cost, sint, ws):
    s, t, _ = h.shape
    tm = min(TOKEN_TILE, t)
    tk = DSA_TK
    per_token = shift.shape[1] != 1
    mod = _tile_spec(tm, D_MODEL, per_token)
    tab = pl.BlockSpec((tm, LANES), lambda b, i: (i, 0))
    tabt = pl.BlockSpec((LANES, tm), lambda b, i: (0, i))
    tok = [(D_CONV, F32), (ATT_DIM, BF16), (IDX_HEADS * IDX_DIM, BF16), (IDX_HEADS, F32),
           (D_MODEL, BF16), (D_MODEL, BF16)]
    feat = lambda rows, dt: (jax.ShapeDtypeStruct((s, rows, t), dt),
                             pl.BlockSpec((1, rows, tm), lambda b, i: (b, 0, i)))
    chunked = lambda rows: (jax.ShapeDtypeStruct((s, t // tk, rows, tk), BF16),
                            pl.BlockSpec((1, tm // tk, rows, tk), lambda b, i: (b, i, 0, 0)))
    key_side = [feat(ATT_DIM, F32), chunked(ATT_DIM), feat(ATT_DIM, F32), chunked(ATT_DIM),
                chunked(LANES), feat(IDX_DIM, F32), feat(N_HEADS, F32)]
    return pl.pallas_call(
        functools.partial(_inproj_kernel, tm=tm, tk=tk),
        out_shape=[jax.ShapeDtypeStruct((s, t, w), dt) for w, dt in tok] + [sd for sd, _ in key_side],
        grid=(s, t // tm),
        in_specs=[_tile_spec(tm, D_MODEL, True), mod, mod, _const_spec((1, D_MODEL)), tab, tab, tabt, tabt]
                 + [_const_spec(w.shape) for w in ws],
        out_specs=[_tile_spec(tm, w, True) for w, _ in tok] + [sp for _, sp in key_side],
        scratch_shapes=[pltpu.VMEM((tm, D_MODEL), BF16)],
        compiler_params=_params(2),
        name="inproj",
    )(h, shift, scale, g, cos, sin, cost, sint, *ws)


def _ln_silu(y, g, b):
    mu = jnp.mean(y, axis=-1, keepdims=True)
    yc = y - mu
    var = jnp.mean(yc * yc, axis=-1, keepdims=True)
    y = yc * lax.rsqrt(var + EPS) * g + b
    return y * _sigmoid(y)


def _conv_kernel(u_ref, up_ref, wdw_ref, bdw_ref, lng_ref, lnb_ref, wco_ref, o_ref, ext_ref, sh_ref, y_ref, *, tm):
    i = pl.program_id(1)
    ext_ref[0:CONV_HALO, :] = jnp.where(i > 0, up_ref[0], 0.0)
    ext_ref[CONV_HALO:, :] = u_ref[0]
    n_sh = tm + CONV_HALO - SUBLANES
    for s in range(1, SUBLANES):
        sh_ref[s - 1] = ext_ref[s:s + n_sh, :]
    off = CONV_HALO - (CONV_W - 1)
    for r in range(tm // CONV_ROWS):
        acc = jnp.broadcast_to(bdw_ref[...], (CONV_ROWS, D_CONV))
        for j in range(CONV_W):
            s = (off + j) % SUBLANES
            lo = r * CONV_ROWS + off + j - s
            x = ext_ref[lo:lo + CONV_ROWS, :] if s == 0 else sh_ref[s - 1, lo:lo + CONV_ROWS, :]
            acc = acc + wdw_ref[j:j + 1, :] * x
        y_ref[r * CONV_ROWS:(r + 1) * CONV_ROWS, :] = _ln_silu(acc, lng_ref[...], lnb_ref[...]).astype(BF16)
    o_ref[0] = _dot(y_ref[...], wco_ref[...]).astype(BF16)


def _conv_prompt(u, wdw, bdw, lng, lnb, wco):
    b, t, _ = u.shape
    tm = min(TOKEN_TILE, t)
    halo_blocks = tm // CONV_HALO
    return pl.pallas_call(
        functools.partial(_conv_kernel, tm=tm),
        out_shape=jax.ShapeDtypeStruct((b, t, D_MODEL), BF16),
        grid=(b, t // tm),
        in_specs=[pl.BlockSpec((1, tm, D_CONV), lambda s, i: (s, i, 0)),
                  pl.BlockSpec((1, CONV_HALO, D_CONV), lambda s, i: (s, jnp.maximum(i * halo_blocks - 1, 0), 0)),
                  _const_spec(wdw.shape), _const_spec(bdw.shape), _const_spec(lng.shape), _const_spec(lnb.shape),
                  _const_spec(wco.shape)],
        out_specs=pl.BlockSpec((1, tm, D_MODEL), lambda s, i: (s, i, 0)),
        scratch_shapes=[pltpu.VMEM((tm + CONV_HALO, D_CONV), F32),
                        pltpu.VMEM((SUBLANES - 1, tm + CONV_HALO - SUBLANES, D_CONV), F32),
                        pltpu.VMEM((tm, D_CONV), BF16)],
        compiler_params=_params(2),
        name="conv_prompt",
    )(u, u, wdw, bdw, lng, lnb, wco)


def _conv_sample_kernel(ext_ref, wdw_ref, bdw_ref, lng_ref, lnb_ref, wco_ref, o_ref, *, steps, rows):
    for s in range(steps):
        acc = jnp.broadcast_to(bdw_ref[...], (rows, D_CONV))
        for j in range(CONV_W):
            acc = acc + wdw_ref[j:j + 1, :] * ext_ref[s + j]
        y = _ln_silu(acc, lng_ref[...], lnb_ref[...]).astype(BF16)
        o_ref[s] = _dot(y, wco_ref[...]).astype(BF16)


def _conv_sample(ext_t, wdw, bdw, lng, lnb, wco):
    n, rows, _ = ext_t.shape
    steps = n - (CONV_W - 1)
    return pl.pallas_call(
        functools.partial(_conv_sample_kernel, steps=steps, rows=rows),
        out_shape=jax.ShapeDtypeStruct((steps, rows, D_MODEL), BF16),
        grid=(1,),
        in_specs=[_const_spec(ext_t.shape), _const_spec(wdw.shape), _const_spec(bdw.shape),
                  _const_spec(lng.shape), _const_spec(lnb.shape), _const_spec(wco.shape)],
        out_specs=pl.BlockSpec((steps, rows, D_MODEL), lambda i: (0, 0, 0)),
        compiler_params=_params(1),
        name="conv_sample",
    )(ext_t, wdw, bdw, lng, lnb, wco)


def _count_term(kind, scores, kpos, thr, pos):
    if kind == "ge":
        return jnp.where(scores >= thr, 1, 0)
    if kind == "gt":
        return jnp.where(scores > thr, 1, 0)
    if kind == "eq":
        return jnp.where(scores == thr, 1, 0)
    return jnp.where(scores == thr, jnp.where(kpos < pos, 1, 0), 0)


def _select_threshold(count, shape, bits=1):
    assert 32 % bits == 0

    def pass_body(p, key):
        shift = 32 - bits * (p + 1)
        digit = jnp.zeros(shape, I32)
        for i in range(1, 2 ** bits):
            cand = key + lax.shift_left(jnp.int32(i), shift)
            digit = digit + jnp.where(count("ge", _key_to_f32(cand), None) >= TOPK_MAX, 1, 0)
        return key + lax.shift_left(digit, shift)

    key = lax.fori_loop(0, 32 // bits, pass_body, jnp.full(shape, INT_MIN, I32))
    thr = jnp.where(key < KEY_NEG_INF, -jnp.inf, _key_to_f32(key))
    need = TOPK_MAX - count("gt", thr, None)
    n_eq = count("eq", thr, None)
    return thr, need, n_eq


def _tie_position(count, thr, need, n_bits):
    def bit_body(b, j):
        cand = j + lax.shift_left(jnp.int32(1), n_bits - 1 - b)
        return jnp.where(count("tie", thr, cand) < need, cand, j)

    return lax.fori_loop(0, n_bits, bit_body, jnp.zeros(thr.shape, I32))


def _bias_from_scores(scores, kpos, qpos, thr, last_tie):
    t1 = jnp.where(kpos <= last_tie, 0.0, NEG)
    t2 = jnp.where(scores == thr, t1, NEG)
    t3 = jnp.where(scores > thr, 0.0, t2)
    return jnp.where(kpos <= qpos, t3, NEG)


def _split_heads(x, lo):
    zero = jnp.zeros_like(x)
    return jnp.where(lo, x, zero), jnp.where(lo, zero, x)


def _dsa_prompt_kernel(qi_ref, wi_ref, q_ref, kid_ref, kb_ref, vb_ref, kn_ref, o_ref,
                       qim_ref, qm_ref, wib_ref, sc_ref, s8_ref, s_ref, p_ref, cand_ref, candt_ref, drop_ref, sel_ref,
                       thr_ref, pos_ref, tie_ref, shift_ref, m_ref, l_ref, al_ref, acc_ref, *, tq, tk, n_bits):
    st = DSA_STRIP
    n_st = tq // st
    n_lt = tk // LANES
    q0 = pl.program_id(1) * tq
    n_ch = (q0 + tq + tk - 1) // tk
    lo = lax.broadcasted_iota(I32, (tq, LANES), 1) < HEAD_DIM
    strips = [slice(r * st, (r + 1) * st) for r in range(n_st)]
    tiles = [slice(t * LANES, (t + 1) * LANES) for t in range(n_lt)]
    lane_s = lax.broadcasted_iota(I32, (st, LANES), 1)
    row_s = lax.broadcasted_iota(I32, (st, LANES), 0)

    for pr in range(N_HEADS // 2):
        sl = slice(pr * LANES, (pr + 1) * LANES)
        qim_ref[2 * pr], qim_ref[2 * pr + 1] = _split_heads(qi_ref[0, :, sl], lo)
        qm_ref[pr, 0:tq], qm_ref[pr, tq:2 * tq] = _split_heads(q_ref[0, :, sl], lo)
    for h in range(IDX_HEADS):
        wib_ref[h] = jnp.broadcast_to(wi_ref[0, :, h:h + 1], (tq, LANES))
    cand_ref[...] = jnp.full(cand_ref.shape, -jnp.inf, F32)
    drop_ref[...] = jnp.full(drop_ref.shape, -jnp.inf, F32)

    def score_chunk(c, carry):
        kc = kid_ref[0, c]
        for h in range(IDX_HEADS):
            s8_ref[h] = _dot(qim_ref[h], kc)
        for r, rows in enumerate(strips):
            acc = [jnp.zeros((st, LANES), F32) for _ in tiles]
            for h in range(IDX_HEADS):
                w = wib_ref[h, rows, :]
                for t, tl in enumerate(tiles):
                    acc[t] = acc[t] + w * jnp.maximum(s8_ref[h, rows, tl], 0.0)
            xs = []
            for t, tl in enumerate(tiles):
                visible = lane_s + (c * tk + t * LANES) <= row_s + (q0 + r * st)
                xs.append(jnp.where(visible, acc[t], NEG))
                sc_ref[c, rows, tl] = xs[t]
            for j in range(DSA_CANDS):
                kept = cand_ref[j, rows, :]
                for t in range(n_lt):
                    kept, xs[t] = jnp.maximum(kept, xs[t]), jnp.minimum(kept, xs[t])
                cand_ref[j, rows, :] = kept
            dropped = drop_ref[rows, :]
            for x in xs:
                dropped = jnp.maximum(dropped, x)
            drop_ref[rows, :] = dropped
        return carry

    lax.fori_loop(0, n_ch, score_chunk, 0)

    def count(kind, thr, pos):
        thr_ref[...] = jnp.broadcast_to(thr, (tq, LANES))
        if pos is not None:
            pos_ref[...] = jnp.broadcast_to(pos, (tq, LANES))

        def body(c, accs):
            out = []
            for r, rows in enumerate(strips):
                thr_b = thr_ref[rows, :]
                pos_b = pos_ref[rows, :] if pos is not None else None
                a = accs[r]
                for t, tl in enumerate(tiles):
                    kpos = lane_s + (c * tk + t * LANES)
                    a = a + _count_term(kind, sc_ref[c, rows, tl], kpos, thr_b, pos_b)
                out.append(a)
            return tuple(out)

        accs = lax.fori_loop(0, n_ch, body, tuple(jnp.zeros((st, LANES), I32) for _ in strips))
        return jnp.concatenate([jnp.sum(a.astype(F32), axis=1, keepdims=True) for a in accs], axis=0)

    for j in range(DSA_CANDS):
        candt_ref[j * LANES:(j + 1) * LANES, :] = cand_ref[j].T

    def count_kept(kind, thr, pos):
        thr_b = jnp.broadcast_to(thr, (8, tq))
        accs = [jnp.zeros((8, tq), I32) for _ in range(4)]
        for g in range(DSA_CANDS * LANES // 8):
            term = _count_term(kind, candt_ref[g * 8:(g + 1) * 8, :], None, thr_b, None)
            accs[g % 4] = accs[g % 4] + term
        total = (accs[0] + accs[1]) + (accs[2] + accs[3])
        return jnp.sum(total.astype(F32), axis=0, keepdims=True)

    def to_cols(x):
        eye = lax.broadcasted_iota(I32, (LANES, LANES), 0) == lax.broadcasted_iota(I32, (LANES, LANES), 1)
        blocks = []
        for b in range(tq // LANES):
            xb = jnp.broadcast_to(x[:, b * LANES:(b + 1) * LANES], (LANES, LANES))
            col = jnp.sum(jnp.where(eye, xb, 0.0), axis=1, keepdims=True)
            blocks.append(jnp.broadcast_to(col, (LANES, LANES)))
        return jnp.concatenate(blocks, axis=0)

    for v, x in enumerate(_select_threshold(count_kept, (1, tq))):
        sel_ref[v] = to_cols(x)
    dropped = drop_ref[...]
    unsafe = jnp.where(dropped > -jnp.inf, jnp.where(dropped >= sel_ref[0], 1.0, 0.0), 0.0)

    @pl.when(jnp.max(unsafe) > 0.5)
    def _():
        for v, x in enumerate(_select_threshold(count, (tq, 1))):
            sel_ref[v] = jnp.broadcast_to(x, (tq, LANES))

    thr, need, n_eq = (sel_ref[v, :, 0:1] for v in range(3))
    tie_ref[...] = jnp.full((tq, LANES), INT_MAX, I32)
    split = n_eq > need

    @pl.when(jnp.max(jnp.where(split, 1.0, 0.0)) > 0.5)
    def _():
        j = _tie_position(count, thr, need, n_bits)
        tie_ref[...] = jnp.broadcast_to(jnp.where(split, j, INT_MAX), (tq, LANES))

    thr_ref[...] = jnp.broadcast_to(thr, (tq, LANES))

    def to_bias(c, carry):
        for r, rows in enumerate(strips):
            thr_b, tie_b = thr_ref[rows, :], tie_ref[rows, :]
            qpos = row_s + (q0 + r * st)
            for t, tl in enumerate(tiles):
                kpos = lane_s + (c * tk + t * LANES)
                sc_ref[c, rows, tl] = _bias_from_scores(sc_ref[c, rows, tl], kpos, qpos, thr_b, tie_b)
        return carry

    lax.fori_loop(0, n_ch, to_bias, 0)

    l_ref[...] = jnp.zeros(l_ref.shape, F32)
    acc_ref[...] = jnp.zeros(acc_ref.shape, F32)

    kn = kn_ref[0]
    visible = lax.broadcasted_iota(I32, kn.shape, 1) < q0 + tq
    kmax = jnp.sqrt(jnp.max(jnp.where(visible, kn, 0.0), axis=1, keepdims=True))
    for h in range(N_HEADS):
        qh = qm_ref[h // 2, (h % 2) * tq:(h % 2 + 1) * tq, :].astype(F32)
        qn = jnp.sqrt(jnp.sum(qh * qh, axis=1, keepdims=True))
        shift_ref[h] = jnp.broadcast_to(qn * kmax[h:h + 1, :], (tq, LANES))
    bounded = jnp.max(shift_ref[...]) <= SHIFT_LIMIT

    def attend_chunk_bounded(c, carry):
        for pr in range(N_HEADS // 2):
            feat = slice(pr * LANES, (pr + 1) * LANES)
            s_ref[pr] = _dot(qm_ref[pr], kb_ref[0, c, feat, :])
            for e in range(2):
                h = 2 * pr + e
                for r, rows in enumerate(strips):
                    srows = slice(e * tq + r * st, e * tq + (r + 1) * st)
                    shift = shift_ref[h, rows, :]
                    psum = l_ref[h, rows, :]
                    for tl in tiles:
                        p = jnp.exp(s_ref[pr, srows, tl] + sc_ref[c, rows, tl] - shift)
                        psum = psum + p
                        p_ref[pr, srows, tl] = p.astype(BF16)
                    l_ref[h, rows, :] = psum
            pv = _dot_nt(p_ref[pr], vb_ref[0, c, feat, :])
            acc_ref[pr] = acc_ref[pr] + jnp.where(lo, pv[:tq], pv[tq:])
        return carry

    @pl.when(bounded)
    def _():
        lax.fori_loop(0, n_ch, attend_chunk_bounded, 0)

    def attend_chunk(c, carry):
        for pr in range(N_HEADS // 2):
            feat = slice(pr * LANES, (pr + 1) * LANES)
            s_ref[pr] = _dot(qm_ref[pr], kb_ref[0, c, feat, :])
            for e in range(2):
                h = 2 * pr + e
                for r, rows in enumerate(strips):
                    srows = slice(e * tq + r * st, e * tq + (r + 1) * st)
                    s = [s_ref[pr, srows, tl] + sc_ref[c, rows, tl] for tl in tiles]
                    mx = s[0]
                    for x in s[1:]:
                        mx = jnp.maximum(mx, x)
                    m_prev = m_ref[h, rows, :]
                    m_new = jnp.maximum(m_prev, jnp.max(mx, axis=1, keepdims=True))
                    alpha = jnp.exp(m_prev - m_new)
                    p = [jnp.exp(x - m_new) for x in s]
                    psum = p[0]
                    for x in p[1:]:
                        psum = psum + x
                    l_ref[h, rows, :] = alpha * l_ref[h, rows, :] + psum
                    m_ref[h, rows, :] = m_new
                    al_ref[e, rows, :] = alpha
                    for x, tl in zip(p, tiles):
                        p_ref[pr, srows, tl] = x.astype(BF16)
            pv = _dot_nt(p_ref[pr], vb_ref[0, c, feat, :])
            a = acc_ref[pr]
            acc_ref[pr] = jnp.where(lo, a * al_ref[0] + pv[:tq], a * al_ref[1] + pv[tq:])
        return carry

    @pl.when(jnp.logical_not(bounded))
    def _():
        m_ref[...] = jnp.full(m_ref.shape, NEG, F32)
        lax.fori_loop(0, n_ch, attend_chunk, 0)

    for pr in range(N_HEADS // 2):
        l_e = jnp.sum(l_ref[2 * pr], axis=1, keepdims=True)
        l_o = jnp.sum(l_ref[2 * pr + 1], axis=1, keepdims=True)
        o_ref[0, :, pr * LANES:(pr + 1) * LANES] = (acc_ref[pr] * jnp.where(lo, 1.0 / l_e, 1.0 / l_o)).astype(BF16)


def _dsa_prompt(qi, wi, q, kidt, kbt, vbt, knorm):
    b, t, _ = q.shape
    tq, tk = DSA_TQ, DSA_TK
    assert tk % tq == 0 and t % tk == 0
    n_bits = max(1, (t - 1).bit_length())
    qspec = lambda w: pl.BlockSpec((1, tq, w), lambda s, i: (s, i, 0))
    kspec = lambda rows: pl.BlockSpec((1, t // tk, rows, tk), lambda s, i: (s, 0, 0, 0),
                                      pipeline_mode=pl.Buffered(1))
    return pl.pallas_call(
        functools.partial(_dsa_prompt_kernel, tq=tq, tk=tk, n_bits=n_bits),
        out_shape=jax.ShapeDtypeStruct((b, t, ATT_DIM), BF16),
        grid=(b, t // tq),
        in_specs=[qspec(IDX_HEADS * IDX_DIM), qspec(IDX_HEADS), qspec(ATT_DIM),
                  kspec(LANES), kspec(ATT_DIM), kspec(ATT_DIM),
                  pl.BlockSpec((1, N_HEADS, t), lambda s, i: (s, 0, 0), pipeline_mode=pl.Buffered(1))],
        out_specs=qspec(ATT_DIM),
        scratch_shapes=[pltpu.VMEM((IDX_HEADS, tq, LANES), BF16),
                        pltpu.VMEM((N_HEADS // 2, 2 * tq, LANES), BF16),
                        pltpu.VMEM((IDX_HEADS, tq, LANES), F32),
                        pltpu.VMEM((t // tk, tq, tk), F32),
                        pltpu.VMEM((IDX_HEADS, tq, tk), F32),
                        pltpu.VMEM((N_HEADS // 2, 2 * tq, tk), F32),
                        pltpu.VMEM((N_HEADS // 2, 2 * tq, tk), BF16),
                        pltpu.VMEM((DSA_CANDS, tq, LANES), F32),
                        pltpu.VMEM((DSA_CANDS * LANES, tq), F32),
                        pltpu.VMEM((tq, LANES), F32),
                        pltpu.VMEM((3, tq, LANES), F32),
                        pltpu.VMEM((tq, LANES), F32),
                        pltpu.VMEM((tq, LANES), I32),
                        pltpu.VMEM((tq, LANES), I32),
                        pltpu.VMEM((N_HEADS, tq, LANES), F32),
                        pltpu.VMEM((N_HEADS, tq, LANES), F32),
                        pltpu.VMEM((N_HEADS, tq, LANES), F32),
                        pltpu.VMEM((2, tq, LANES), F32),
                        pltpu.VMEM((N_HEADS // 2, tq, LANES), F32)],
        compiler_params=_params(2),
        name="dsa_prompt",
    )(qi, wi, q, kidt, kbt, vbt, knorm)


def _dsa_sample_kernel(pt_ref, qi_ref, wi_ref, q_ref, kidn_ref, kn_ref, vn_ref, *rest, n_pages, rows, past, group):
    n_pg = group * n_pages
    idx_refs = rest[:n_pg]
    k_refs = rest[n_pg:2 * n_pg]
    v_refs = rest[2 * n_pg:3 * n_pg]
    o_ref, kid_all, k_all, v_all, tie_ref = rest[3 * n_pg:]
    n_keys = past + PAGE_SIZE
    tail = slice(past, n_keys)
    lo = lax.broadcasted_iota(I32, (rows, LANES), 1) < HEAD_DIM

    accs = []
    for g in range(group):
        for p in range(n_pages):
            sl = slice(p * PAGE_SIZE, (p + 1) * PAGE_SIZE)
            x = idx_refs[g * n_pages + p][0]
            kid_all[g, :, sl] = jnp.concatenate([x, x], axis=0).astype(BF16)
            k_all[g, :, sl] = k_refs[g * n_pages + p][0].astype(BF16)
            v_all[g, :, sl] = v_refs[g * n_pages + p][0].astype(BF16)
        kid_all[g, :, tail] = kidn_ref[g]
        k_all[g, :, tail] = kn_ref[g]
        v_all[g, :, tail] = vn_ref[g]

        qi = qi_ref[g]
        parts = []
        for pr in range(IDX_HEADS // 2):
            parts.extend(_split_heads(qi[:, pr * LANES:(pr + 1) * LANES], lo))
        s_all = _dot(jnp.concatenate(parts, axis=0).astype(BF16), kid_all[g])
        wi = wi_ref[g]
        acc = jnp.zeros((rows, n_keys), F32)
        for h in range(IDX_HEADS):
            acc = acc + wi[:, h:h + 1] * jnp.maximum(s_all[h * rows:(h + 1) * rows], 0.0)
        accs.append(acc)

    n_q = group * rows
    kpos = lax.broadcasted_iota(I32, (n_q, n_keys), 1)
    qpos = (lax.broadcasted_iota(I32, (n_q, n_keys), 0) & (rows - 1)) + past
    scores = jnp.where(kpos <= qpos, jnp.concatenate(accs, axis=0), NEG)

    def count(kind, thr, pos):
        return jnp.sum(_count_term(kind, scores, kpos, thr, pos).astype(F32), axis=1, keepdims=True)

    n_bits = (n_keys - 1).bit_length()
    thr, need, n_eq = _select_threshold(count, (n_q, 1), DECODE_BITS)
    tie_ref[...] = jnp.full((n_q, 1), INT_MAX, I32)
    split = n_eq > need

    @pl.when(jnp.max(jnp.where(split, 1.0, 0.0)) > 0.5)
    def _():
        tie_ref[...] = jnp.where(split, _tie_position(count, thr, need, n_bits), INT_MAX)

    bias = _bias_from_scores(scores, kpos, qpos, thr, tie_ref[...])

    logits = []
    for g in range(group):
        bias_g = bias[g * rows:(g + 1) * rows]
        bias2 = jnp.concatenate([bias_g, bias_g], axis=0)
        q = q_ref[g]
        for pr in range(N_HEADS // 2):
            sl = slice(pr * LANES, (pr + 1) * LANES)
            lhs = jnp.concatenate(_split_heads(q[:, sl], lo), axis=0).astype(BF16)
            logits.append(_dot(lhs, k_all[g, sl, :]) + bias2)
    s = jnp.concatenate(logits, axis=0)
    p = jnp.exp(s - jnp.max(s, axis=1, keepdims=True))
    l = jnp.sum(p, axis=1, keepdims=True)
    p = p.astype(BF16)
    for g in range(group):
        for pr in range(N_HEADS // 2):
            sl = slice(pr * LANES, (pr + 1) * LANES)
            blk = slice((g * (N_HEADS // 2) + pr) * 2 * rows, (g * (N_HEADS // 2) + pr + 1) * 2 * rows)
            pv = _dot_nt(p[blk], v_all[g, sl, :]) / l[blk]
            o_ref[g, :, sl] = jnp.where(lo, pv[:rows], pv[rows:])


def _dsa_sample(page_table, qi, wi, q, kidn, kn, vn, cache_idx, cache_k, cache_v):
    db, rows, _ = q.shape
    n_pages = page_table.shape[1]
    past = n_pages * PAGE_SIZE
    n_keys = past + PAGE_SIZE
    group = DECODE_GROUP if db % DECODE_GROUP == 0 else 1
    n_pg = group * n_pages
    assert rows & (rows - 1) == 0
    seq = lambda w: pl.BlockSpec((group, rows, w), lambda s, pt: (s, 0, 0))
    new = lambda f: pl.BlockSpec((group, f, PAGE_SIZE), lambda s, pt: (s, 0, 0))

    def page(f, gp):
        return pl.BlockSpec((1, f, PAGE_SIZE), lambda s, pt: (pt[s * n_pg + gp], 0, 0))

    grid_spec = pltpu.PrefetchScalarGridSpec(
        num_scalar_prefetch=1,
        grid=(db // group,),
        in_specs=[seq(IDX_HEADS * IDX_DIM), seq(IDX_HEADS), seq(ATT_DIM), new(LANES), new(ATT_DIM), new(ATT_DIM)]
                 + [page(IDX_DIM, gp) for gp in range(n_pg)]
                 + [page(ATT_DIM, gp) for gp in range(n_pg)]
                 + [page(ATT_DIM, gp) for gp in range(n_pg)],
        out_specs=seq(ATT_DIM),
        scratch_shapes=[pltpu.VMEM((group, LANES, n_keys), BF16), pltpu.VMEM((group, ATT_DIM, n_keys), BF16),
                        pltpu.VMEM((group, ATT_DIM, n_keys), BF16), pltpu.VMEM((group * rows, 1), I32)],
    )
    return pl.pallas_call(
        functools.partial(_dsa_sample_kernel, n_pages=n_pages, rows=rows, past=past, group=group),
        out_shape=jax.ShapeDtypeStruct((db, rows, ATT_DIM), F32),
        grid_spec=grid_spec,
        compiler_params=_params(1),
        name="dsa_sample",
    )(page_table.reshape(-1), qi, wi, q, kidn, kn, vn,
      *([cache_idx] * n_pg), *([cache_k] * n_pg), *([cache_v] * n_pg))


def _out_kernel(h_ref, att_ref, conv_ref, sgc_ref, sga_ref, g2_ref, sh_ref, sc_ref, g3_ref, gn_ref, gfin_ref,
                wao_ref, wo_ref, wg_ref, wu_ref, wd_ref, y_ref, hn_ref, acc_ref):
    att_o = _dot(att_ref[0], wao_ref[...])
    mix = sgc_ref[0].astype(F32) * conv_ref[0].astype(F32) + sga_ref[0].astype(F32) * att_o
    h2 = h_ref[0] + g2_ref[0] * _dot(mix.astype(BF16), wo_ref[...])
    y_ref[0] = h2
    hn_ref[...] = _rms_mod(h2, gn_ref[...], sc_ref[0], sh_ref[0]).astype(BF16)
    _swiglu_into(hn_ref, wg_ref, wu_ref, wd_ref, acc_ref)
    h3 = y_ref[0] + 0.5 * g3_ref[0] * acc_ref[...]
    ms = jnp.mean(h3 * h3, axis=-1, keepdims=True)
    y_ref[0] = h3 * lax.rsqrt(ms + EPS) * gfin_ref[...]


def _out(h, att, conv, sgc, sga, g2, shift, scale, g3, gn, gfin, wao, wo, wg, wu, wd):
    s, t, _ = h.shape
    tm = min(TOKEN_TILE, t)
    per_token = shift.shape[1] != 1
    mod = _tile_spec(tm, D_MODEL, per_token)
    tok = lambda w: _tile_spec(tm, w, True)
    return pl.pallas_call(
        _out_kernel,
        out_shape=jax.ShapeDtypeStruct(h.shape, F32),
        grid=(s, t // tm),
        in_specs=[tok(D_MODEL), tok(ATT_DIM), tok(D_MODEL), tok(D_MODEL), tok(D_MODEL), mod, mod, mod, mod,
                  _const_spec((1, D_MODEL)), _const_spec((1, D_MODEL)),
                  _const_spec(wao.shape), _const_spec(wo.shape),
                  _const_spec(wg.shape), _const_spec(wu.shape), _const_spec(wd.shape)],
        out_specs=tok(D_MODEL),
        scratch_shapes=[pltpu.VMEM((tm, D_MODEL), BF16), pltpu.VMEM((tm, D_MODEL), F32)],
        compiler_params=_params(2),
        name="out_ffn2",
    )(h, att, conv, sgc, sga, g2, shift, scale, g3, gn, gfin, wao, wo, wg, wu, wd)


def _rope_tables(pos):
    inv = ROPE_THETA ** (-jnp.arange(0, HEAD_DIM, 2, dtype=F32) / HEAD_DIM)
    ang = pos.astype(F32)[:, None] * inv[None, :]
    cos, sin = jnp.cos(ang), jnp.sin(ang)
    cos = jnp.concatenate([cos] * 4, axis=1)
    sin = jnp.concatenate([-sin, sin, -sin, sin], axis=1)
    return cos, sin, cos.T, sin.T


def _ff_weights(wg, wu, wd):
    split_cols = lambda w: w.reshape(D_MODEL, N_FF_CHUNKS, FF_CHUNK).transpose(1, 0, 2).astype(BF16)
    return split_cols(wg), split_cols(wu), wd.reshape(N_FF_CHUNKS, FF_CHUNK, D_MODEL).astype(BF16)


def _in_weights(w_in):
    cuts = [0]
    for width in (D_CONV, D_CONV, ATT_DIM, ATT_DIM, ATT_DIM, IDX_HEADS * IDX_DIM, IDX_DIM, IDX_HEADS,
                  D_MODEL, D_MODEL):
        cuts.append(cuts[-1] + width)
    piece = lambda a, b: w_in[:, cuts[a]:cuts[b]]
    w_qw = jnp.concatenate([piece(5, 6), piece(7, 8), jnp.zeros((D_MODEL, LANES - IDX_HEADS), F32)], axis=1)
    w_ki = piece(6, 7)
    w_kvt = jnp.concatenate([piece(3, 5), w_ki, w_ki], axis=1).T
    return [w.astype(BF16) for w in (piece(0, 2), piece(2, 3), w_qw, piece(8, 10), w_kvt)]


def kernel(x_prompt, x_sample, cache_k, cache_v, cache_idx_k, state_conv, page_table, c_prompt, c_sample,
           w_ada, b_ada, g_ffn1, w1_gate, w1_up, w1_down, g_mix, w_in, w_dw, b_dw, ln_g, ln_b,
           w_conv_out, w_attn_o, w_out, g_ffn2, w2_gate, w2_up, w2_down, g_final):
    bsz, t, _ = x_prompt.shape
    db, s, _ = x_sample.shape
    depth = w_ada.shape[0]
    n_pool = cache_k.shape[1]
    past = page_table.shape[1] * PAGE_SIZE
    q_rows = 8
    assert depth == 1 and t >= CONV_W - 1 and s <= q_rows

    rope_p = _rope_tables(jnp.arange(t))
    rope_s = _rope_tables(jnp.tile(past + jnp.arange(s), db))
    row = lambda v: v.reshape(1, -1)

    c_all = jnp.concatenate([c_prompt, c_sample], axis=0)
    n_c = c_all.shape[0]
    c_all = jnp.pad(c_all, ((0, -n_c % 8), (0, 0)))

    hp = x_prompt
    hs = x_sample.reshape(1, db * s, D_MODEL)
    outs = [[] for _ in range(8)]
    for l in range(depth):
        m = _ada(c_all, w_ada[l], b_ada[l])
        mod_p = [m[:bsz, i * D_MODEL:(i + 1) * D_MODEL].reshape(bsz, 1, D_MODEL) for i in range(N_MOD)]
        mod_s = [jnp.repeat(m[bsz:n_c, i * D_MODEL:(i + 1) * D_MODEL], s, axis=0).reshape(1, db * s, D_MODEL)
                 for i in range(N_MOD)]
        ff1 = _ff_weights(w1_gate[l], w1_up[l], w1_down[l])
        ff2 = _ff_weights(w2_gate[l], w2_up[l], w2_down[l])
        w_ins = _in_weights(w_in[l])
        conv_w = (w_dw[l], row(b_dw[l]), row(ln_g[l]), row(ln_b[l]), w_conv_out[l].astype(BF16))
        wao, wo = w_attn_o[l].astype(BF16), w_out[l].astype(BF16)

        def front(h, mod, rope):
            h = _ffn(h, mod[0], mod[1], mod[2], row(g_ffn1[l]), *ff1)
            return h, _inproj(h, mod[3], mod[4], row(g_mix[l]), *rope, w_ins)

        def back(h, att, conv, sgc, sga, mod):
            return _out(h, att, conv, sgc, sga, mod[5], mod[6], mod[7], mod[8], row(g_ffn2[l]), row(g_final),
                        wao, wo, *ff2)

        hp, (u, q, qi, wi, sgc, sga, kt, kbt, vt, vbt, kidt, kit, knorm) = front(hp, mod_p, rope_p)
        conv = _conv_prompt(u, *conv_w)
        att = _dsa_prompt(qi, wi, q, kidt, kbt, vbt, knorm)
        hp = back(hp, att, conv, sgc, sga, mod_p)
        heads_last = lambda a: a.reshape(bsz, N_HEADS, HEAD_DIM, t).transpose(0, 3, 1, 2)
        outs[0].append(heads_last(kt))
        outs[1].append(heads_last(vt))
        outs[2].append(kit.transpose(0, 2, 1))
        outs[3].append(u[:, t - (CONV_W - 1):])

        hs, (u, q, qi, wi, sgc, sga, kt, kbt, vt, vbt, kidt, kit, _) = front(hs, mod_s, rope_s)
        per_seq = lambda a: jnp.pad(a.astype(F32).reshape(db, s, -1), ((0, 0), (0, q_rows - s), (0, 0)))

        def new_keys(a):
            f = a.shape[2]
            a = a[0].transpose(1, 0, 2).reshape(f, db, s).transpose(1, 0, 2)
            return jnp.pad(a, ((0, 0), (0, 0), (0, PAGE_SIZE - s)))

        ext_t = jnp.concatenate([state_conv[l].transpose(1, 0, 2), u.reshape(db, s, D_CONV).transpose(1, 0, 2)],
                                axis=0)
        conv = _conv_sample(ext_t, *conv_w).transpose(1, 0, 2).reshape(1, db * s, D_MODEL)
        att = _dsa_sample(page_table, per_seq(qi), per_seq(wi), per_seq(q),
                          new_keys(kidt), new_keys(kbt), new_keys(vbt),
                          cache_idx_k[l].transpose(0, 2, 1),
                          cache_k[l].transpose(0, 2, 3, 1).reshape(n_pool, ATT_DIM, PAGE_SIZE),
                          cache_v[l].transpose(0, 2, 3, 1).reshape(n_pool, ATT_DIM, PAGE_SIZE))
        att = att[:, :s].reshape(1, db * s, ATT_DIM).astype(BF16)
        hs = back(hs, att, conv, sgc, sga, mod_s)
        heads_last = lambda a: a.reshape(N_HEADS, HEAD_DIM, db, s).transpose(2, 3, 0, 1)
        outs[4].append(heads_last(kt))
        outs[5].append(heads_last(vt))
        outs[6].append(kit.reshape(IDX_DIM, db, s).transpose(1, 2, 0))
        outs[7].append(ext_t[s:].transpose(1, 0, 2))

    return (hp, hs.reshape(db, s, D_MODEL)) + tuple(jnp.stack(o) for o in outs)
```

```python
import functools

import jax
import jax.numpy as jnp
from jax import lax
from jax.experimental import pallas as pl
from jax.experimental.pallas import tpu as pltpu

F32 = jnp.float32
BF16 = jnp.bfloat16
I32 = jnp.int32

D_MODEL = 1024
N_HEADS = 8
HEAD_DIM = 64
ATT_DIM = N_HEADS * HEAD_DIM
IDX_HEADS = 8
IDX_DIM = 64
D_CONV = D_MODEL // 2
CONV_W = 31
D_FF = ((8 * D_MODEL // 3 + 255) // 256) * 256
N_MOD = 9
TOPK_MAX = 256
PAGE_SIZE = 128
ROPE_THETA = 10000.0
EPS = 1e-6
NEG = -1e30

LANES = 128
SUBLANES = 8
FF_CHUNK = 256
N_FF_CHUNKS = D_FF // FF_CHUNK
TOKEN_TILE = 512
CONV_HALO = 32
CONV_ROWS = 64
DSA_TQ = 256
DSA_TK = 512
DSA_STRIP = 64
DSA_CANDS = 12
DECODE_BITS = 4
DECODE_GROUP = 2
SHIFT_LIMIT = 40.0
VMEM_LIMIT = 56 * 1024 * 1024
INT_MIN = -2 ** 31
INT_MAX = 2 ** 31 - 1
KEY_NEG_INF = INT_MIN + 2 ** 23 - 1


def _dot(a, b):
    return jnp.dot(a, b, preferred_element_type=F32)


def _dot_nt(a, b):
    return lax.dot_general(a, b, (((1,), (1,)), ((), ())), preferred_element_type=F32)


def _sigmoid(x):
    return jax.nn.sigmoid(x)


def _rms_mod(x, g, scale, shift):
    ms = jnp.mean(x * x, axis=-1, keepdims=True)
    return (x * lax.rsqrt(ms + EPS)) * g * (1.0 + scale) + shift


def _key_to_f32(key):
    return pltpu.bitcast(key ^ ((key >> 31) & INT_MAX), F32)


def _const_spec(shape):
    nd = len(shape)
    return pl.BlockSpec(shape, lambda *_: (0,) * nd, pipeline_mode=pl.Buffered(1))


def _params(n_grid):
    return pltpu.CompilerParams(dimension_semantics=("arbitrary",) * n_grid, vmem_limit_bytes=VMEM_LIMIT)


def _ada_kernel(c_ref, w_ref, b_ref, o_ref):
    c = c_ref[...]
    s = (c * _sigmoid(c)).astype(BF16)
    o_ref[...] = _dot(s, w_ref[...].astype(BF16)) + b_ref[...]


def _ada(c, w, b):
    rows = c.shape[0]
    nb = D_MODEL
    return pl.pallas_call(
        _ada_kernel,
        out_shape=jax.ShapeDtypeStruct((rows, N_MOD * D_MODEL), F32),
        grid=(N_MOD * D_MODEL // nb,),
        in_specs=[pl.BlockSpec((rows, D_MODEL), lambda j: (0, 0)),
                  pl.BlockSpec((D_MODEL, nb), lambda j: (0, j)),
                  pl.BlockSpec((1, nb), lambda j: (0, j))],
        out_specs=pl.BlockSpec((rows, nb), lambda j: (0, j)),
        compiler_params=_params(1),
        name="ada",
    )(c, w, b.reshape(1, -1))


def _swiglu_into(hn_ref, wg_ref, wu_ref, wd_ref, acc_ref):
    acc_ref[...] = jnp.zeros_like(acc_ref)

    def body(i, carry):
        hn = hn_ref[...]
        a = _dot(hn, wg_ref[i])
        b = _dot(hn, wu_ref[i])
        act = (a * _sigmoid(a) * b).astype(BF16)
        acc_ref[...] += _dot(act, wd_ref[i])
        return carry

    lax.fori_loop(0, N_FF_CHUNKS, body, 0)


def _ffn_kernel(x_ref, sh_ref, sc_ref, gt_ref, g_ref, wg_ref, wu_ref, wd_ref, o_ref, hn_ref, acc_ref):
    x = x_ref[0]
    hn_ref[...] = _rms_mod(x, g_ref[...], sc_ref[0], sh_ref[0]).astype(BF16)
    _swiglu_into(hn_ref, wg_ref, wu_ref, wd_ref, acc_ref)
    o_ref[0] = x + 0.5 * gt_ref[0] * acc_ref[...]


def _tile_spec(tm, width, per_token):
    if per_token:
        return pl.BlockSpec((1, tm, width), lambda b, i: (b, i, 0))
    return pl.BlockSpec((1, 1, width), lambda b, i: (b, 0, 0))


def _ffn(x, shift, scale, gate, g, wg, wu, wd):
    s, t, _ = x.shape
    tm = min(TOKEN_TILE, t)
    per_token = shift.shape[1] != 1
    tok = _tile_spec(tm, D_MODEL, True)
    mod = _tile_spec(tm, D_MODEL, per_token)
    return pl.pallas_call(
        _ffn_kernel,
        out_shape=jax.ShapeDtypeStruct(x.shape, F32),
        grid=(s, t // tm),
        in_specs=[tok, mod, mod, mod, _const_spec((1, D_MODEL)),
                  _const_spec(wg.shape), _const_spec(wu.shape), _const_spec(wd.shape)],
        out_specs=tok,
        scratch_shapes=[pltpu.VMEM((tm, D_MODEL), BF16), pltpu.VMEM((tm, D_MODEL), F32)],
        compiler_params=_params(2),
        name="ffn1",
    )(x, shift, scale, gate, g, wg, wu, wd)


def _rope(x, cos, sin, lo32):
    sw = jnp.where(lo32, pltpu.roll(x, 96, 1), pltpu.roll(x, 32, 1))
    return x * cos + sw * sin


def _rope_t(x, cos, sin):
    h = HEAD_DIM // 2
    sw = jnp.concatenate([x[h:2 * h], x[0:h], x[3 * h:4 * h], x[2 * h:3 * h]], axis=0)
    return x * cos + sw * sin


def _inproj_kernel(h_ref, sh_ref, sc_ref, g_ref, cos_ref, sin_ref, cost_ref, sint_ref,
                   wglu_ref, wq_ref, wqw_ref, wgt_ref, wkvt_ref,
                   u_ref, q_ref, qi_ref, wi_ref, sgc_ref, sga_ref,
                   kt_ref, kbt_ref, vt_ref, vbt_ref, kidt_ref, kit_ref, kn_ref, n_ref, *, tm, tk):
    n_ref[...] = _rms_mod(h_ref[0], g_ref[...], sc_ref[0], sh_ref[0]).astype(BF16)
    lane = lax.broadcasted_iota(I32, (1, LANES), 1)
    lo32 = (lane & (HEAD_DIM - 1)) < HEAD_DIM // 2
    rope = functools.partial(_rope, cos=cos_ref[...], sin=sin_ref[...], lo32=lo32)
    n_q = ATT_DIM // LANES

    z = _dot(n_ref[...], wglu_ref[...])
    u_ref[0] = z[:, :D_CONV] * _sigmoid(z[:, D_CONV:])

    z = _dot(n_ref[...], wq_ref[...])
    for s in range(n_q):
        sl = slice(s * LANES, (s + 1) * LANES)
        q_ref[0, :, sl] = (rope(z[:, sl]) * HEAD_DIM ** -0.5).astype(BF16)

    z = _dot(n_ref[...], wqw_ref[...])
    for s in range(n_q):
        sl = slice(s * LANES, (s + 1) * LANES)
        qi_ref[0, :, sl] = rope(z[:, sl]).astype(BF16)
    wi_ref[0] = z[:, ATT_DIM:ATT_DIM + IDX_HEADS] * (IDX_HEADS * IDX_DIM) ** -0.5

    z = _dot(n_ref[...], wgt_ref[...])
    sgc_ref[0] = _sigmoid(z[:, :D_MODEL]).astype(BF16)
    sga_ref[0] = _sigmoid(z[:, D_MODEL:]).astype(BF16)

    zt = _dot_nt(wkvt_ref[...], n_ref[...])
    cost, sint = cost_ref[...], sint_ref[...]
    chunks = [slice(j * tk, (j + 1) * tk) for j in range(tm // tk)]
    norms = []
    for s in range(n_q):
        rows = slice(s * LANES, (s + 1) * LANES)
        xk = _rope_t(zt[rows], cost, sint)
        kt_ref[0, rows, :] = xk
        xkb = xk.astype(BF16)
        for j, ch in enumerate(chunks):
            kbt_ref[0, j, rows, :] = xkb[:, ch]
        sq = xkb.astype(F32)
        sq = sq * sq
        norms += [jnp.sum(sq[:HEAD_DIM], axis=0, keepdims=True), jnp.sum(sq[HEAD_DIM:], axis=0, keepdims=True)]
    kn_ref[0] = jnp.concatenate(norms, axis=0)
    zv = zt[ATT_DIM:2 * ATT_DIM]
    vt_ref[0] = zv
    for j, ch in enumerate(chunks):
        vbt_ref[0, j] = zv[:, ch].astype(BF16)
    kid = _rope_t(zt[2 * ATT_DIM:], cost, sint)
    kit_ref[0] = kid[:IDX_DIM]
    for j, ch in enumerate(chunks):
        kidt_ref[0, j] = kid[:, ch].astype(BF16)


def _inproj(h, shift, scale, g, cos, sin, cost, sint, ws):
    s, t, _ = h.shape
    tm = min(TOKEN_TILE, t)
    tk = DSA_TK
    per_token = shift.shape[1] != 1
    mod = _tile_spec(tm, D_MODEL, per_token)
    tab = pl.BlockSpec((tm, LANES), lambda b, i: (i, 0))
    tabt = pl.BlockSpec((LANES, tm), lambda b, i: (0, i))
    tok = [(D_CONV, F32), (ATT_DIM, BF16), (IDX_HEADS * IDX_DIM, BF16), (IDX_HEADS, F32),
           (D_MODEL, BF16), (D_MODEL, BF16)]
    feat = lambda rows, dt: (jax.ShapeDtypeStruct((s, rows, t), dt),
                             pl.BlockSpec((1, rows, tm), lambda b, i: (b, 0, i)))
    chunked = lambda rows: (jax.ShapeDtypeStruct((s, t // tk, rows, tk), BF16),
                            pl.BlockSpec((1, tm // tk, rows, tk), lambda b, i: (b, i, 0, 0)))
    key_side = [feat(ATT_DIM, F32), chunked(ATT_DIM), feat(ATT_DIM, F32), chunked(ATT_DIM),
                chunked(LANES), feat(IDX_DIM, F32), feat(N_HEADS, F32)]
    return pl.pallas_call(
        functools.partial(_inproj_kernel, tm=tm, tk=tk),
        out_shape=[jax.ShapeDtypeStruct((s, t, w), dt) for w, dt in tok] + [sd for sd, _ in key_side],
        grid=(s, t // tm),
        in_specs=[_tile_spec(tm, D_MODEL, True), mod, mod, _const_spec((1, D_MODEL)), tab, tab, tabt, tabt]
                 + [_const_spec(w.shape) for w in ws],
        out_specs=[_tile_spec(tm, w, True) for w, _ in tok] + [sp for _, sp in key_side],
        scratch_shapes=[pltpu.VMEM((tm, D_MODEL), BF16)],
        compiler_params=_params(2),
        name="inproj",
    )(h, shift, scale, g, cos, sin, cost, sint, *ws)


def _ln_silu(y, g, b):
    mu = jnp.mean(y, axis=-1, keepdims=True)
    yc = y - mu
    var = jnp.mean(yc * yc, axis=-1, keepdims=True)
    y = yc * lax.rsqrt(var + EPS) * g + b
    return y * _sigmoid(y)


def _conv_kernel(u_ref, up_ref, wdw_ref, bdw_ref, lng_ref, lnb_ref, wco_ref, o_ref, ext_ref, sh_ref, y_ref, *, tm):
    i = pl.program_id(1)
    ext_ref[0:CONV_HALO, :] = jnp.where(i > 0, up_ref[0], 0.0)
    ext_ref[CONV_HALO:, :] = u_ref[0]
    n_sh = tm + CONV_HALO - SUBLANES
    for s in range(1, SUBLANES):
        sh_ref[s - 1] = ext_ref[s:s + n_sh, :]
    off = CONV_HALO - (CONV_W - 1)
    for r in range(tm // CONV_ROWS):
        acc = jnp.broadcast_to(bdw_ref[...], (CONV_ROWS, D_CONV))
        for j in range(CONV_W):
            s = (off + j) % SUBLANES
            lo = r * CONV_ROWS + off + j - s
            x = ext_ref[lo:lo + CONV_ROWS, :] if s == 0 else sh_ref[s - 1, lo:lo + CONV_ROWS, :]
            acc = acc + wdw_ref[j:j + 1, :] * x
        y_ref[r * CONV_ROWS:(r + 1) * CONV_ROWS, :] = _ln_silu(acc, lng_ref[...], lnb_ref[...]).astype(BF16)
    o_ref[0] = _dot(y_ref[...], wco_ref[...]).astype(BF16)


def _conv_prompt(u, wdw, bdw, lng, lnb, wco):
    b, t, _ = u.shape
    tm = min(TOKEN_TILE, t)
    halo_blocks = tm // CONV_HALO
    return pl.pallas_call(
        functools.partial(_conv_kernel, tm=tm),
        out_shape=jax.ShapeDtypeStruct((b, t, D_MODEL), BF16),
        grid=(b, t // tm),
        in_specs=[pl.BlockSpec((1, tm, D_CONV), lambda s, i: (s, i, 0)),
                  pl.BlockSpec((1, CONV_HALO, D_CONV), lambda s, i: (s, jnp.maximum(i * halo_blocks - 1, 0), 0)),
                  _const_spec(wdw.shape), _const_spec(bdw.shape), _const_spec(lng.shape), _const_spec(lnb.shape),
                  _const_spec(wco.shape)],
        out_specs=pl.BlockSpec((1, tm, D_MODEL), lambda s, i: (s, i, 0)),
        scratch_shapes=[pltpu.VMEM((tm + CONV_HALO, D_CONV), F32),
                        pltpu.VMEM((SUBLANES - 1, tm + CONV_HALO - SUBLANES, D_CONV), F32),
                        pltpu.VMEM((tm, D_CONV), BF16)],
        compiler_params=_params(2),
        name="conv_prompt",
    )(u, u, wdw, bdw, lng, lnb, wco)


def _conv_sample_kernel(ext_ref, wdw_ref, bdw_ref, lng_ref, lnb_ref, wco_ref, o_ref, *, steps, rows):
    for s in range(steps):
        acc = jnp.broadcast_to(bdw_ref[...], (rows, D_CONV))
        for j in range(CONV_W):
            acc = acc + wdw_ref[j:j + 1, :] * ext_ref[s + j]
        y = _ln_silu(acc, lng_ref[...], lnb_ref[...]).astype(BF16)
        o_ref[s] = _dot(y, wco_ref[...]).astype(BF16)


def _conv_sample(ext_t, wdw, bdw, lng, lnb, wco):
    n, rows, _ = ext_t.shape
    steps = n - (CONV_W - 1)
    return pl.pallas_call(
        functools.partial(_conv_sample_kernel, steps=steps, rows=rows),
        out_shape=jax.ShapeDtypeStruct((steps, rows, D_MODEL), BF16),
        grid=(1,),
        in_specs=[_const_spec(ext_t.shape), _const_spec(wdw.shape), _const_spec(bdw.shape),
                  _const_spec(lng.shape), _const_spec(lnb.shape), _const_spec(wco.shape)],
        out_specs=pl.BlockSpec((steps, rows, D_MODEL), lambda i: (0, 0, 0)),
        compiler_params=_params(1),
        name="conv_sample",
    )(ext_t, wdw, bdw, lng, lnb, wco)


def _count_term(kind, scores, kpos, thr, pos):
    if kind == "ge":
        return jnp.where(scores >= thr, 1, 0)
    if kind == "gt":
        return jnp.where(scores > thr, 1, 0)
    if kind == "eq":
        return jnp.where(scores == thr, 1, 0)
    return jnp.where(scores == thr, jnp.where(kpos < pos, 1, 0), 0)


def _select_threshold(count, shape, bits=1):
    assert 32 % bits == 0

    def pass_body(p, key):
        shift = 32 - bits * (p + 1)
        digit = jnp.zeros(shape, I32)
        for i in range(1, 2 ** bits):
            cand = key + lax.shift_left(jnp.int32(i), shift)
            digit = digit + jnp.where(count("ge", _key_to_f32(cand), None) >= TOPK_MAX, 1, 0)
        return key + lax.shift_left(digit, shift)

    key = lax.fori_loop(0, 32 // bits, pass_body, jnp.full(shape, INT_MIN, I32))
    thr = jnp.where(key < KEY_NEG_INF, -jnp.inf, _key_to_f32(key))
    need = TOPK_MAX - count("gt", thr, None)
    n_eq = count("eq", thr, None)
    return thr, need, n_eq


def _tie_position(count, thr, need, n_bits):
    def bit_body(b, j):
        cand = j + lax.shift_left(jnp.int32(1), n_bits - 1 - b)
        return jnp.where(count("tie", thr, cand) < need, cand, j)

    return lax.fori_loop(0, n_bits, bit_body, jnp.zeros(thr.shape, I32))


def _bias_from_scores(scores, kpos, qpos, thr, last_tie):
    t1 = jnp.where(kpos <= last_tie, 0.0, NEG)
    t2 = jnp.where(scores == thr, t1, NEG)
    t3 = jnp.where(scores > thr, 0.0, t2)
    return jnp.where(kpos <= qpos, t3, NEG)


def _split_heads(x, lo):
    zero = jnp.zeros_like(x)
    return jnp.where(lo, x, zero), jnp.where(lo, zero, x)


def _dsa_prompt_kernel(qi_ref, wi_ref, q_ref, kid_ref, kb_ref, vb_ref, kn_ref, qin_ref, win_ref, o_ref,
                       qim_ref, qm_ref, wib_ref, sc_ref, s8_ref, s_ref, p_ref, cand_ref, candt_ref, drop_ref, sel_ref,
                       thr_ref, pos_ref, tie_ref, shift_ref, m_ref, l_ref, al_ref, acc_ref, *, tq, tk, n_bits):
    st = DSA_STRIP
    n_st = tq // st
    n_lt = tk // LANES
    blk = pl.program_id(1)
    slot = blk % 2
    has_next = blk + 1 < pl.num_programs(1)
    q0 = blk * tq
    n_ch = (q0 + tq + tk - 1) // tk
    n_ch_next = (q0 + 2 * tq + tk - 1) // tk
    lo = lax.broadcasted_iota(I32, (tq, LANES), 1) < HEAD_DIM
    strips = [slice(r * st, (r + 1) * st) for r in range(n_st)]
    tiles = [slice(t * LANES, (t + 1) * LANES) for t in range(n_lt)]
    lane_s = lax.broadcasted_iota(I32, (st, LANES), 1)
    row_s = lax.broadcasted_iota(I32, (st, LANES), 0)

    def prepare_scoring(qi_blk, wi_blk):
        for pr in range(N_HEADS // 2):
            sl = slice(pr * LANES, (pr + 1) * LANES)
            qim_ref[2 * pr], qim_ref[2 * pr + 1] = _split_heads(qi_blk[0, :, sl], lo)
        for h in range(IDX_HEADS):
            wib_ref[h] = jnp.broadcast_to(wi_blk[0, :, h:h + 1], (tq, LANES))
        cand_ref[...] = jnp.full(cand_ref.shape, -jnp.inf, F32)
        drop_ref[...] = jnp.full(drop_ref.shape, -jnp.inf, F32)

    def score_chunk(c, dst, qbase):
        kc = kid_ref[0, c]
        for h in range(IDX_HEADS):
            s8_ref[h] = _dot(qim_ref[h], kc)
        for r, rows in enumerate(strips):
            acc = [jnp.zeros((st, LANES), F32) for _ in tiles]
            for h in range(IDX_HEADS):
                w = wib_ref[h, rows, :]
                for t, tl in enumerate(tiles):
                    acc[t] = acc[t] + w * jnp.maximum(s8_ref[h, rows, tl], 0.0)
            xs = []
            for t, tl in enumerate(tiles):
                visible = lane_s + (c * tk + t * LANES) <= row_s + (qbase + r * st)
                xs.append(jnp.where(visible, acc[t], NEG))
                sc_ref[dst, c, rows, tl] = xs[t]
            for j in range(DSA_CANDS):
                kept = cand_ref[j, rows, :]
                for t in range(n_lt):
                    kept, xs[t] = jnp.maximum(kept, xs[t]), jnp.minimum(kept, xs[t])
                cand_ref[j, rows, :] = kept
            dropped = drop_ref[rows, :]
            for x in xs:
                dropped = jnp.maximum(dropped, x)
            drop_ref[rows, :] = dropped

    def score_next(c, carry):
        score_chunk(c, 1 - slot, q0 + tq)
        return carry

    @pl.when(blk == 0)
    def _():
        prepare_scoring(qi_ref, wi_ref)

        def score_first(c, carry):
            score_chunk(c, slot, q0)
            return carry

        lax.fori_loop(0, n_ch, score_first, 0)

    for pr in range(N_HEADS // 2):
        sl = slice(pr * LANES, (pr + 1) * LANES)
        qm_ref[pr, 0:tq], qm_ref[pr, tq:2 * tq] = _split_heads(q_ref[0, :, sl], lo)

    def count(kind, thr, pos):
        thr_ref[...] = jnp.broadcast_to(thr, (tq, LANES))
        if pos is not None:
            pos_ref[...] = jnp.broadcast_to(pos, (tq, LANES))

        def body(c, accs):
            out = []
            for r, rows in enumerate(strips):
                thr_b = thr_ref[rows, :]
                pos_b = pos_ref[rows, :] if pos is not None else None
                a = accs[r]
                for t, tl in enumerate(tiles):
                    kpos = lane_s + (c * tk + t * LANES)
                    a = a + _count_term(kind, sc_ref[slot, c, rows, tl], kpos, thr_b, pos_b)
                out.append(a)
            return tuple(out)

        accs = lax.fori_loop(0, n_ch, body, tuple(jnp.zeros((st, LANES), I32) for _ in strips))
        return jnp.concatenate([jnp.sum(a.astype(F32), axis=1, keepdims=True) for a in accs], axis=0)

    for j in range(DSA_CANDS):
        candt_ref[j * LANES:(j + 1) * LANES, :] = cand_ref[j].T

    def count_kept(kind, thr, pos):
        thr_b = jnp.broadcast_to(thr, (8, tq))
        accs = [jnp.zeros((8, tq), I32) for _ in range(4)]
        for g in range(DSA_CANDS * LANES // 8):
            term = _count_term(kind, candt_ref[g * 8:(g + 1) * 8, :], None, thr_b, None)
            accs[g % 4] = accs[g % 4] + term
        total = (accs[0] + accs[1]) + (accs[2] + accs[3])
        return jnp.sum(total.astype(F32), axis=0, keepdims=True)

    def to_cols(x):
        eye = lax.broadcasted_iota(I32, (LANES, LANES), 0) == lax.broadcasted_iota(I32, (LANES, LANES), 1)
        blocks = []
        for b in range(tq // LANES):
            xb = jnp.broadcast_to(x[:, b * LANES:(b + 1) * LANES], (LANES, LANES))
            col = jnp.sum(jnp.where(eye, xb, 0.0), axis=1, keepdims=True)
            blocks.append(jnp.broadcast_to(col, (LANES, LANES)))
        return jnp.concatenate(blocks, axis=0)

    for v, x in enumerate(_select_threshold(count_kept, (1, tq))):
        sel_ref[v] = to_cols(x)
    dropped = drop_ref[...]
    unsafe = jnp.where(dropped > -jnp.inf, jnp.where(dropped >= sel_ref[0], 1.0, 0.0), 0.0)

    @pl.when(jnp.max(unsafe) > 0.5)
    def _():
        for v, x in enumerate(_select_threshold(count, (tq, 1))):
            sel_ref[v] = jnp.broadcast_to(x, (tq, LANES))

    thr, need, n_eq = (sel_ref[v, :, 0:1] for v in range(3))
    tie_ref[...] = jnp.full((tq, LANES), INT_MAX, I32)
    split = n_eq > need

    @pl.when(jnp.max(jnp.where(split, 1.0, 0.0)) > 0.5)
    def _():
        j = _tie_position(count, thr, need, n_bits)
        tie_ref[...] = jnp.broadcast_to(jnp.where(split, j, INT_MAX), (tq, LANES))

    thr_ref[...] = jnp.broadcast_to(thr, (tq, LANES))

    def to_bias(c, carry):
        for r, rows in enumerate(strips):
            thr_b, tie_b = thr_ref[rows, :], tie_ref[rows, :]
            qpos = row_s + (q0 + r * st)
            for t, tl in enumerate(tiles):
                kpos = lane_s + (c * tk + t * LANES)
                sc_ref[slot, c, rows, tl] = _bias_from_scores(sc_ref[slot, c, rows, tl], kpos, qpos, thr_b, tie_b)
        return carry

    lax.fori_loop(0, n_ch, to_bias, 0)

    @pl.when(has_next)
    def _():
        prepare_scoring(qin_ref, win_ref)

    l_ref[...] = jnp.zeros(l_ref.shape, F32)
    acc_ref[...] = jnp.zeros(acc_ref.shape, F32)

    kn = kn_ref[0]
    visible = lax.broadcasted_iota(I32, kn.shape, 1) < q0 + tq
    kmax = jnp.sqrt(jnp.max(jnp.where(visible, kn, 0.0), axis=1, keepdims=True))
    for h in range(N_HEADS):
        qh = qm_ref[h // 2, (h % 2) * tq:(h % 2 + 1) * tq, :].astype(F32)
        qn = jnp.sqrt(jnp.sum(qh * qh, axis=1, keepdims=True))
        shift_ref[h] = jnp.broadcast_to(qn * kmax[h:h + 1, :], (tq, LANES))
    bounded = jnp.max(shift_ref[...]) <= SHIFT_LIMIT

    def attend_chunk_bounded(c, carry):
        for pr in range(N_HEADS // 2):
            feat = slice(pr * LANES, (pr + 1) * LANES)
            s_ref[pr % 2] = _dot(qm_ref[pr], kb_ref[0, c, feat, :])
            for e in range(2):
                h = 2 * pr + e
                for r, rows in enumerate(strips):
                    srows = slice(e * tq + r * st, e * tq + (r + 1) * st)
                    shift = shift_ref[h, rows, :]
                    psum = l_ref[h, rows, :]
                    for tl in tiles:
                        p = jnp.exp(s_ref[pr % 2, srows, tl] + sc_ref[slot, c, rows, tl] - shift)
                        psum = psum + p
                        p_ref[pr % 2, srows, tl] = p.astype(BF16)
                    l_ref[h, rows, :] = psum
            pv = _dot_nt(p_ref[pr % 2], vb_ref[0, c, feat, :])
            acc_ref[pr] = acc_ref[pr] + jnp.where(lo, pv[:tq], pv[tq:])
        return carry

    def attend_and_score_next(c, carry):
        attend_chunk_bounded(c, carry)
        return score_next(c, carry)

    @pl.when(jnp.logical_and(bounded, has_next))
    def _():
        lax.fori_loop(0, n_ch, attend_and_score_next, 0)

    @pl.when(jnp.logical_and(bounded, jnp.logical_not(has_next)))
    def _():
        lax.fori_loop(0, n_ch, attend_chunk_bounded, 0)

    def attend_chunk(c, carry):
        for pr in range(N_HEADS // 2):
            feat = slice(pr * LANES, (pr + 1) * LANES)
            s_ref[pr % 2] = _dot(qm_ref[pr], kb_ref[0, c, feat, :])
            for e in range(2):
                h = 2 * pr + e
                for r, rows in enumerate(strips):
                    srows = slice(e * tq + r * st, e * tq + (r + 1) * st)
                    s = [s_ref[pr % 2, srows, tl] + sc_ref[slot, c, rows, tl] for tl in tiles]
                    mx = s[0]
                    for x in s[1:]:
                        mx = jnp.maximum(mx, x)
                    m_prev = m_ref[h, rows, :]
                    m_new = jnp.maximum(m_prev, jnp.max(mx, axis=1, keepdims=True))
                    alpha = jnp.exp(m_prev - m_new)
                    p = [jnp.exp(x - m_new) for x in s]
                    psum = p[0]
                    for x in p[1:]:
                        psum = psum + x
                    l_ref[h, rows, :] = alpha * l_ref[h, rows, :] + psum
                    m_ref[h, rows, :] = m_new
                    al_ref[e, rows, :] = alpha
                    for x, tl in zip(p, tiles):
                        p_ref[pr % 2, srows, tl] = x.astype(BF16)
            pv = _dot_nt(p_ref[pr % 2], vb_ref[0, c, feat, :])
            a = acc_ref[pr]
            acc_ref[pr] = jnp.where(lo, a * al_ref[0] + pv[:tq], a * al_ref[1] + pv[tq:])
        return carry

    @pl.when(jnp.logical_not(bounded))
    def _():
        m_ref[...] = jnp.full(m_ref.shape, NEG, F32)
        lax.fori_loop(0, n_ch, attend_chunk, 0)

    @pl.when(has_next)
    def _():
        first = jnp.where(bounded, n_ch, 0)
        lax.fori_loop(first, n_ch_next, score_next, 0)

    for pr in range(N_HEADS // 2):
        l_e = jnp.sum(l_ref[2 * pr], axis=1, keepdims=True)
        l_o = jnp.sum(l_ref[2 * pr + 1], axis=1, keepdims=True)
        o_ref[0, :, pr * LANES:(pr + 1) * LANES] = (acc_ref[pr] * jnp.where(lo, 1.0 / l_e, 1.0 / l_o)).astype(BF16)


def _dsa_prompt(qi, wi, q, kidt, kbt, vbt, knorm):
    b, t, _ = q.shape
    tq, tk = DSA_TQ, DSA_TK
    assert tk % tq == 0 and t % tk == 0
    n_bits = max(1, (t - 1).bit_length())
    last = t // tq - 1
    qspec = lambda w: pl.BlockSpec((1, tq, w), lambda s, i: (s, i, 0))
    qnext = lambda w: pl.BlockSpec((1, tq, w), lambda s, i: (s, jnp.minimum(i + 1, last), 0))
    kspec = lambda rows: pl.BlockSpec((1, t // tk, rows, tk), lambda s, i: (s, 0, 0, 0),
                                      pipeline_mode=pl.Buffered(1))
    return pl.pallas_call(
        functools.partial(_dsa_prompt_kernel, tq=tq, tk=tk, n_bits=n_bits),
        out_shape=jax.ShapeDtypeStruct((b, t, ATT_DIM), BF16),
        grid=(b, t // tq),
        in_specs=[qspec(IDX_HEADS * IDX_DIM), qspec(IDX_HEADS), qspec(ATT_DIM),
                  kspec(LANES), kspec(ATT_DIM), kspec(ATT_DIM),
                  pl.BlockSpec((1, N_HEADS, t), lambda s, i: (s, 0, 0), pipeline_mode=pl.Buffered(1)),
                  qnext(IDX_HEADS * IDX_DIM), qnext(IDX_HEADS)],
        out_specs=qspec(ATT_DIM),
        scratch_shapes=[pltpu.VMEM((IDX_HEADS, tq, LANES), BF16),
                        pltpu.VMEM((N_HEADS // 2, 2 * tq, LANES), BF16),
                        pltpu.VMEM((IDX_HEADS, tq, LANES), F32),
                        pltpu.VMEM((2, t // tk, tq, tk), F32),
                        pltpu.VMEM((IDX_HEADS, tq, tk), F32),
                        pltpu.VMEM((2, 2 * tq, tk), F32),
                        pltpu.VMEM((2, 2 * tq, tk), BF16),
                        pltpu.VMEM((DSA_CANDS, tq, LANES), F32),
                        pltpu.VMEM((DSA_CANDS * LANES, tq), F32),
                        pltpu.VMEM((tq, LANES), F32),
                        pltpu.VMEM((3, tq, LANES), F32),
                        pltpu.VMEM((tq, LANES), F32),
                        pltpu.VMEM((tq, LANES), I32),
                        pltpu.VMEM((tq, LANES), I32),
                        pltpu.VMEM((N_HEADS, tq, LANES), F32),
                        pltpu.VMEM((N_HEADS, tq, LANES), F32),
                        pltpu.VMEM((N_HEADS, tq, LANES), F32),
                        pltpu.VMEM((2, tq, LANES), F32),
                        pltpu.VMEM((N_HEADS // 2, tq, LANES), F32)],
        compiler_params=_params(2),
        name="dsa_prompt",
    )(qi, wi, q, kidt, kbt, vbt, knorm, qi, wi)


def _dsa_sample_kernel(pt_ref, qi_ref, wi_ref, q_ref, kidn_ref, kn_ref, vn_ref, *rest, n_pages, rows, past, group):
    n_pg = group * n_pages
    idx_refs = rest[:n_pg]
    k_refs = rest[n_pg:2 * n_pg]
    v_refs = rest[2 * n_pg:3 * n_pg]
    o_ref, kid_all, k_all, v_all, tie_ref = rest[3 * n_pg:]
    n_keys = past + PAGE_SIZE
    tail = slice(past, n_keys)
    lo = lax.broadcasted_iota(I32, (rows, LANES), 1) < HEAD_DIM

    accs = []
    for g in range(group):
        for p in range(n_pages):
            sl = slice(p * PAGE_SIZE, (p + 1) * PAGE_SIZE)
            x = idx_refs[g * n_pages + p][0]
            kid_all[g, :, sl] = jnp.concatenate([x, x], axis=0).astype(BF16)
            k_all[g, :, sl] = k_refs[g * n_pages + p][0].astype(BF16)
            v_all[g, :, sl] = v_refs[g * n_pages + p][0].astype(BF16)
        kid_all[g, :, tail] = kidn_ref[g]
        k_all[g, :, tail] = kn_ref[g]
        v_all[g, :, tail] = vn_ref[g]

        qi = qi_ref[g]
        parts = []
        for pr in range(IDX_HEADS // 2):
            parts.extend(_split_heads(qi[:, pr * LANES:(pr + 1) * LANES], lo))
        s_all = _dot(jnp.concatenate(parts, axis=0).astype(BF16), kid_all[g])
        wi = wi_ref[g]
        acc = jnp.zeros((rows, n_keys), F32)
        for h in range(IDX_HEADS):
            acc = acc + wi[:, h:h + 1] * jnp.maximum(s_all[h * rows:(h + 1) * rows], 0.0)
        accs.append(acc)

    n_q = group * rows
    kpos = lax.broadcasted_iota(I32, (n_q, n_keys), 1)
    qpos = (lax.broadcasted_iota(I32, (n_q, n_keys), 0) & (rows - 1)) + past
    scores = jnp.where(kpos <= qpos, jnp.concatenate(accs, axis=0), NEG)

    def count(kind, thr, pos):
        return jnp.sum(_count_term(kind, scores, kpos, thr, pos).astype(F32), axis=1, keepdims=True)

    n_bits = (n_keys - 1).bit_length()
    thr, need, n_eq = _select_threshold(count, (n_q, 1), DECODE_BITS)
    tie_ref[...] = jnp.full((n_q, 1), INT_MAX, I32)
    split = n_eq > need

    @pl.when(jnp.max(jnp.where(split, 1.0, 0.0)) > 0.5)
    def _():
        tie_ref[...] = jnp.where(split, _tie_position(count, thr, need, n_bits), INT_MAX)

    bias = _bias_from_scores(scores, kpos, qpos, thr, tie_ref[...])

    logits = []
    for g in range(group):
        bias_g = bias[g * rows:(g + 1) * rows]
        bias2 = jnp.concatenate([bias_g, bias_g], axis=0)
        q = q_ref[g]
        for pr in range(N_HEADS // 2):
            sl = slice(pr * LANES, (pr + 1) * LANES)
            lhs = jnp.concatenate(_split_heads(q[:, sl], lo), axis=0).astype(BF16)
            logits.append(_dot(lhs, k_all[g, sl, :]) + bias2)
    s = jnp.concatenate(logits, axis=0)
    p = jnp.exp(s - jnp.max(s, axis=1, keepdims=True))
    l = jnp.sum(p, axis=1, keepdims=True)
    p = p.astype(BF16)
    for g in range(group):
        for pr in range(N_HEADS // 2):
            sl = slice(pr * LANES, (pr + 1) * LANES)
            blk = slice((g * (N_HEADS // 2) + pr) * 2 * rows, (g * (N_HEADS // 2) + pr + 1) * 2 * rows)
            pv = _dot_nt(p[blk], v_all[g, sl, :]) / l[blk]
            o_ref[g, :, sl] = jnp.where(lo, pv[:rows], pv[rows:])


def _dsa_sample(page_table, qi, wi, q, kidn, kn, vn, cache_idx, cache_k, cache_v):
    db, rows, _ = q.shape
    n_pages = page_table.shape[1]
    past = n_pages * PAGE_SIZE
    n_keys = past + PAGE_SIZE
    group = DECODE_GROUP if db % DECODE_GROUP == 0 else 1
    n_pg = group * n_pages
    assert rows & (rows - 1) == 0
    seq = lambda w: pl.BlockSpec((group, rows, w), lambda s, pt: (s, 0, 0))
    new = lambda f: pl.BlockSpec((group, f, PAGE_SIZE), lambda s, pt: (s, 0, 0))

    def page(f, gp):
        return pl.BlockSpec((1, f, PAGE_SIZE), lambda s, pt: (pt[s * n_pg + gp], 0, 0))

    grid_spec = pltpu.PrefetchScalarGridSpec(
        num_scalar_prefetch=1,
        grid=(db // group,),
        in_specs=[seq(IDX_HEADS * IDX_DIM), seq(IDX_HEADS), seq(ATT_DIM), new(LANES), new(ATT_DIM), new(ATT_DIM)]
                 + [page(IDX_DIM, gp) for gp in range(n_pg)]
                 + [page(ATT_DIM, gp) for gp in range(n_pg)]
                 + [page(ATT_DIM, gp) for gp in range(n_pg)],
        out_specs=seq(ATT_DIM),
        scratch_shapes=[pltpu.VMEM((group, LANES, n_keys), BF16), pltpu.VMEM((group, ATT_DIM, n_keys), BF16),
                        pltpu.VMEM((group, ATT_DIM, n_keys), BF16), pltpu.VMEM((group * rows, 1), I32)],
    )
    return pl.pallas_call(
        functools.partial(_dsa_sample_kernel, n_pages=n_pages, rows=rows, past=past, group=group),
        out_shape=jax.ShapeDtypeStruct((db, rows, ATT_DIM), F32),
        grid_spec=grid_spec,
        compiler_params=_params(1),
        name="dsa_sample",
    )(page_table.reshape(-1), qi, wi, q, kidn, kn, vn,
      *([cache_idx] * n_pg), *([cache_k] * n_pg), *([cache_v] * n_pg))


def _out_kernel(h_ref, att_ref, conv_ref, sgc_ref, sga_ref, g2_ref, sh_ref, sc_ref, g3_ref, gn_ref, gfin_ref,
                wao_ref, wo_ref, wg_ref, wu_ref, wd_ref, y_ref, hn_ref, acc_ref):
    att_o = _dot(att_ref[0], wao_ref[...])
    mix = sgc_ref[0].astype(F32) * conv_ref[0].astype(F32) + sga_ref[0].astype(F32) * att_o
    h2 = h_ref[0] + g2_ref[0] * _dot(mix.astype(BF16), wo_ref[...])
    y_ref[0] = h2
    hn_ref[...] = _rms_mod(h2, gn_ref[...], sc_ref[0], sh_ref[0]).astype(BF16)
    _swiglu_into(hn_ref, wg_ref, wu_ref, wd_ref, acc_ref)
    h3 = y_ref[0] + 0.5 * g3_ref[0] * acc_ref[...]
    ms = jnp.mean(h3 * h3, axis=-1, keepdims=True)
    y_ref[0] = h3 * lax.rsqrt(ms + EPS) * gfin_ref[...]


def _out(h, att, conv, sgc, sga, g2, shift, scale, g3, gn, gfin, wao, wo, wg, wu, wd):
    s, t, _ = h.shape
    tm = min(TOKEN_TILE, t)
    per_token = shift.shape[1] != 1
    mod = _tile_spec(tm, D_MODEL, per_token)
    tok = lambda w: _tile_spec(tm, w, True)
    return pl.pallas_call(
        _out_kernel,
        out_shape=jax.ShapeDtypeStruct(h.shape, F32),
        grid=(s, t // tm),
        in_specs=[tok(D_MODEL), tok(ATT_DIM), tok(D_MODEL), tok(D_MODEL), tok(D_MODEL), mod, mod, mod, mod,
                  _const_spec((1, D_MODEL)), _const_spec((1, D_MODEL)),
                  _const_spec(wao.shape), _const_spec(wo.shape),
                  _const_spec(wg.shape), _const_spec(wu.shape), _const_spec(wd.shape)],
        out_specs=tok(D_MODEL),
        scratch_shapes=[pltpu.VMEM((tm, D_MODEL), BF16), pltpu.VMEM((tm, D_MODEL), F32)],
        compiler_params=_params(2),
        name="out_ffn2",
    )(h, att, conv, sgc, sga, g2, shift, scale, g3, gn, gfin, wao, wo, wg, wu, wd)


def _rope_tables(pos):
    inv = ROPE_THETA ** (-jnp.arange(0, HEAD_DIM, 2, dtype=F32) / HEAD_DIM)
    ang = pos.astype(F32)[:, None] * inv[None, :]
    cos, sin = jnp.cos(ang), jnp.sin(ang)
    cos = jnp.concatenate([cos] * 4, axis=1)
    sin = jnp.concatenate([-sin, sin, -sin, sin], axis=1)
    return cos, sin, cos.T, sin.T


def _ff_weights(wg, wu, wd):
    split_cols = lambda w: w.reshape(D_MODEL, N_FF_CHUNKS, FF_CHUNK).transpose(1, 0, 2).astype(BF16)
    return split_cols(wg), split_cols(wu), wd.reshape(N_FF_CHUNKS, FF_CHUNK, D_MODEL).astype(BF16)


def _in_weights(w_in):
    cuts = [0]
    for width in (D_CONV, D_CONV, ATT_DIM, ATT_DIM, ATT_DIM, IDX_HEADS * IDX_DIM, IDX_DIM, IDX_HEADS,
                  D_MODEL, D_MODEL):
        cuts.append(cuts[-1] + width)
    piece = lambda a, b: w_in[:, cuts[a]:cuts[b]]
    w_qw = jnp.concatenate([piece(5, 6), piece(7, 8), jnp.zeros((D_MODEL, LANES - IDX_HEADS), F32)], axis=1)
    w_ki = piece(6, 7)
    w_kvt = jnp.concatenate([piece(3, 5), w_ki, w_ki], axis=1).T
    return [w.astype(BF16) for w in (piece(0, 2), piece(2, 3), w_qw, piece(8, 10), w_kvt)]


def kernel(x_prompt, x_sample, cache_k, cache_v, cache_idx_k, state_conv, page_table, c_prompt, c_sample,
           w_ada, b_ada, g_ffn1, w1_gate, w1_up, w1_down, g_mix, w_in, w_dw, b_dw, ln_g, ln_b,
           w_conv_out, w_attn_o, w_out, g_ffn2, w2_gate, w2_up, w2_down, g_final):
    bsz, t, _ = x_prompt.shape
    db, s, _ = x_sample.shape
    depth = w_ada.shape[0]
    n_pool = cache_k.shape[1]
    past = page_table.shape[1] * PAGE_SIZE
    q_rows = 8
    assert depth == 1 and t >= CONV_W - 1 and s <= q_rows

    rope_p = _rope_tables(jnp.arange(t))
    rope_s = _rope_tables(jnp.tile(past + jnp.arange(s), db))
    row = lambda v: v.reshape(1, -1)

    c_all = jnp.concatenate([c_prompt, c_sample], axis=0)
    n_c = c_all.shape[0]
    c_all = jnp.pad(c_all, ((0, -n_c % 8), (0, 0)))

    hp = x_prompt
    hs = x_sample.reshape(1, db * s, D_MODEL)
    outs = [[] for _ in range(8)]
    for l in range(depth):
        m = _ada(c_all, w_ada[l], b_ada[l])
        mod_p = [m[:bsz, i * D_MODEL:(i + 1) * D_MODEL].reshape(bsz, 1, D_MODEL) for i in range(N_MOD)]
        mod_s = [jnp.repeat(m[bsz:n_c, i * D_MODEL:(i + 1) * D_MODEL], s, axis=0).reshape(1, db * s, D_MODEL)
                 for i in range(N_MOD)]
        ff1 = _ff_weights(w1_gate[l], w1_up[l], w1_down[l])
        ff2 = _ff_weights(w2_gate[l], w2_up[l], w2_down[l])
        w_ins = _in_weights(w_in[l])
        conv_w = (w_dw[l], row(b_dw[l]), row(ln_g[l]), row(ln_b[l]), w_conv_out[l].astype(BF16))
        wao, wo = w_attn_o[l].astype(BF16), w_out[l].astype(BF16)

        def front(h, mod, rope):
            h = _ffn(h, mod[0], mod[1], mod[2], row(g_ffn1[l]), *ff1)
            return h, _inproj(h, mod[3], mod[4], row(g_mix[l]), *rope, w_ins)

        def back(h, att, conv, sgc, sga, mod):
            return _out(h, att, conv, sgc, sga, mod[5], mod[6], mod[7], mod[8], row(g_ffn2[l]), row(g_final),
                        wao, wo, *ff2)

        hp, (u, q, qi, wi, sgc, sga, kt, kbt, vt, vbt, kidt, kit, knorm) = front(hp, mod_p, rope_p)
        conv = _conv_prompt(u, *conv_w)
        att = _dsa_prompt(qi, wi, q, kidt, kbt, vbt, knorm)
        hp = back(hp, att, conv, sgc, sga, mod_p)
        heads_last = lambda a: a.reshape(bsz, N_HEADS, HEAD_DIM, t).transpose(0, 3, 1, 2)
        outs[0].append(heads_last(kt))
        outs[1].append(heads_last(vt))
        outs[2].append(kit.transpose(0, 2, 1))
        outs[3].append(u[:, t - (CONV_W - 1):])

        hs, (u, q, qi, wi, sgc, sga, kt, kbt, vt, vbt, kidt, kit, _) = front(hs, mod_s, rope_s)
        per_seq = lambda a: jnp.pad(a.astype(F32).reshape(db, s, -1), ((0, 0), (0, q_rows - s), (0, 0)))

        def new_keys(a):
            f = a.shape[2]
            a = a[0].transpose(1, 0, 2).reshape(f, db, s).transpose(1, 0, 2)
            return jnp.pad(a, ((0, 0), (0, 0), (0, PAGE_SIZE - s)))

        ext_t = jnp.concatenate([state_conv[l].transpose(1, 0, 2), u.reshape(db, s, D_CONV).transpose(1, 0, 2)],
                                axis=0)
        conv = _conv_sample(ext_t, *conv_w).transpose(1, 0, 2).reshape(1, db * s, D_MODEL)
        att = _dsa_sample(page_table, per_seq(qi), per_seq(wi), per_seq(q),
                          new_keys(kidt), new_keys(kbt), new_keys(vbt),
                          cache_idx_k[l].transpose(0, 2, 1),
                          cache_k[l].transpose(0, 2, 3, 1).reshape(n_pool, ATT_DIM, PAGE_SIZE),
                          cache_v[l].transpose(0, 2, 3, 1).reshape(n_pool, ATT_DIM, PAGE_SIZE))
        att = att[:, :s].reshape(1, db * s, ATT_DIM).astype(BF16)
        hs = back(hs, att, conv, sgc, sga, mod_s)
        heads_last = lambda a: a.reshape(N_HEADS, HEAD_DIM, db, s).transpose(2, 3, 0, 1)
        outs[4].append(heads_last(kt))
        outs[5].append(heads_last(vt))
        outs[6].append(kit.reshape(IDX_DIM, db, s).transpose(1, 2, 0))
        outs[7].append(ext_t[s:].transpose(1, 0, 2))

    return (hp, hs.reshape(db, s, D_MODEL)) + tuple(jnp.stack(o) for o in outs)
```

```python
import functools

import jax
import jax.numpy as jnp
from jax import lax
from jax.experimental import pallas as pl
from jax.experimental.pallas import tpu as pltpu

F32 = jnp.float32
BF16 = jnp.bfloat16
I32 = jnp.int32

D_MODEL = 1024
N_HEADS = 8
HEAD_DIM = 64
ATT_DIM = N_HEADS * HEAD_DIM
IDX_HEADS = 8
IDX_DIM = 64
D_CONV = D_MODEL // 2
CONV_W = 31
D_FF = ((8 * D_MODEL // 3 + 255) // 256) * 256
N_MOD = 9
TOPK_MAX = 256
PAGE_SIZE = 128
ROPE_THETA = 10000.0
EPS = 1e-6
NEG = -1e30

LANES = 128
SUBLANES = 8
FF_CHUNK = 256
N_FF_CHUNKS = D_FF // FF_CHUNK
TOKEN_TILE = 512
CONV_HALO = 32
CONV_ROWS = 64
DSA_TQ = 256
DSA_TK = 512
DSA_STRIP = 64
DSA_CANDS = 12
DECODE_BITS = 4
DECODE_GROUP = 2
SHIFT_LIMIT = 40.0
Q_SCALE = HEAD_DIM ** -0.5 * 1.4426950408889634
VMEM_LIMIT = 56 * 1024 * 1024
INT_MIN = -2 ** 31
INT_MAX = 2 ** 31 - 1
KEY_NEG_INF = INT_MIN + 2 ** 23 - 1


def _dot(a, b):
    return jnp.dot(a, b, preferred_element_type=F32)


def _dot_nt(a, b):
    return lax.dot_general(a, b, (((1,), (1,)), ((), ())), preferred_element_type=F32)


def _sigmoid(x):
    return jax.nn.sigmoid(x)


def _rms_mod(x, g, scale, shift):
    ms = jnp.mean(x * x, axis=-1, keepdims=True)
    return (x * lax.rsqrt(ms + EPS)) * g * (1.0 + scale) + shift


def _key_to_f32(key):
    return pltpu.bitcast(key ^ ((key >> 31) & INT_MAX), F32)


def _const_spec(shape):
    nd = len(shape)
    return pl.BlockSpec(shape, lambda *_: (0,) * nd, pipeline_mode=pl.Buffered(1))


def _params(n_grid):
    return pltpu.CompilerParams(dimension_semantics=("arbitrary",) * n_grid, vmem_limit_bytes=VMEM_LIMIT)


def _ada_kernel(c_ref, w_ref, b_ref, o_ref):
    c = c_ref[...]
    s = (c * _sigmoid(c)).astype(BF16)
    o_ref[...] = _dot(s, w_ref[...].astype(BF16)) + b_ref[...]


def _ada(c, w, b):
    rows = c.shape[0]
    nb = D_MODEL
    return pl.pallas_call(
        _ada_kernel,
        out_shape=jax.ShapeDtypeStruct((rows, N_MOD * D_MODEL), F32),
        grid=(N_MOD * D_MODEL // nb,),
        in_specs=[pl.BlockSpec((rows, D_MODEL), lambda j: (0, 0)),
                  pl.BlockSpec((D_MODEL, nb), lambda j: (0, j)),
                  pl.BlockSpec((1, nb), lambda j: (0, j))],
        out_specs=pl.BlockSpec((rows, nb), lambda j: (0, j)),
        compiler_params=_params(1),
        name="ada",
    )(c, w, b.reshape(1, -1))


def _swiglu_into(hn_ref, wg_ref, wu_ref, wd_ref, acc_ref):
    acc_ref[...] = jnp.zeros_like(acc_ref)

    def body(i, carry):
        hn = hn_ref[...]
        a = _dot(hn, wg_ref[i])
        b = _dot(hn, wu_ref[i])
        act = (a * _sigmoid(a) * b).astype(BF16)
        acc_ref[...] += _dot(act, wd_ref[i])
        return carry

    lax.fori_loop(0, N_FF_CHUNKS, body, 0)


def _ffn_kernel(x_ref, sh_ref, sc_ref, gt_ref, g_ref, wg_ref, wu_ref, wd_ref, o_ref, hn_ref, acc_ref):
    x = x_ref[0]
    hn_ref[...] = _rms_mod(x, g_ref[...], sc_ref[0], sh_ref[0]).astype(BF16)
    _swiglu_into(hn_ref, wg_ref, wu_ref, wd_ref, acc_ref)
    o_ref[0] = x + 0.5 * gt_ref[0] * acc_ref[...]


def _tile_spec(tm, width, per_token):
    if per_token:
        return pl.BlockSpec((1, tm, width), lambda b, i: (b, i, 0))
    return pl.BlockSpec((1, 1, width), lambda b, i: (b, 0, 0))


def _ffn(x, shift, scale, gate, g, wg, wu, wd):
    s, t, _ = x.shape
    tm = min(TOKEN_TILE, t)
    per_token = shift.shape[1] != 1
    tok = _tile_spec(tm, D_MODEL, True)
    mod = _tile_spec(tm, D_MODEL, per_token)
    return pl.pallas_call(
        _ffn_kernel,
        out_shape=jax.ShapeDtypeStruct(x.shape, F32),
        grid=(s, t // tm),
        in_specs=[tok, mod, mod, mod, _const_spec((1, D_MODEL)),
                  _const_spec(wg.shape), _const_spec(wu.shape), _const_spec(wd.shape)],
        out_specs=tok,
        scratch_shapes=[pltpu.VMEM((tm, D_MODEL), BF16), pltpu.VMEM((tm, D_MODEL), F32)],
        compiler_params=_params(2),
        name="ffn1",
    )(x, shift, scale, gate, g, wg, wu, wd)


def _rope(x, cos, sin, lo32):
    sw = jnp.where(lo32, pltpu.roll(x, 96, 1), pltpu.roll(x, 32, 1))
    return x * cos + sw * sin


def _rope_t(x, cos, sin):
    h = HEAD_DIM // 2
    sw = jnp.concatenate([x[h:2 * h], x[0:h], x[3 * h:4 * h], x[2 * h:3 * h]], axis=0)
    return x * cos + sw * sin


def _inproj_kernel(h_ref, sh_ref, sc_ref, g_ref, cos_ref, sin_ref, cost_ref, sint_ref,
                   wglu_ref, wq_ref, wqw_ref, wgt_ref, wkvt_ref,
                   u_ref, q_ref, qi_ref, wi_ref, sgc_ref, sga_ref,
                   kt_ref, kbt_ref, vt_ref, vbt_ref, kidt_ref, kit_ref, kn_ref, n_ref, *, tm, tk):
    n_ref[...] = _rms_mod(h_ref[0], g_ref[...], sc_ref[0], sh_ref[0]).astype(BF16)
    lane = lax.broadcasted_iota(I32, (1, LANES), 1)
    lo32 = (lane & (HEAD_DIM - 1)) < HEAD_DIM // 2
    rope = functools.partial(_rope, cos=cos_ref[...], sin=sin_ref[...], lo32=lo32)
    n_q = ATT_DIM // LANES

    z = _dot(n_ref[...], wglu_ref[...])
    u_ref[0] = z[:, :D_CONV] * _sigmoid(z[:, D_CONV:])

    z = _dot(n_ref[...], wq_ref[...])
    for s in range(n_q):
        sl = slice(s * LANES, (s + 1) * LANES)
        q_ref[0, :, sl] = (rope(z[:, sl]) * Q_SCALE).astype(BF16)

    z = _dot(n_ref[...], wqw_ref[...])
    for s in range(n_q):
        sl = slice(s * LANES, (s + 1) * LANES)
        qi_ref[0, :, sl] = rope(z[:, sl]).astype(BF16)
    wi_ref[0] = z[:, ATT_DIM:ATT_DIM + IDX_HEADS] * (IDX_HEADS * IDX_DIM) ** -0.5

    z = _dot(n_ref[...], wgt_ref[...])
    sgc_ref[0] = _sigmoid(z[:, :D_MODEL]).astype(BF16)
    sga_ref[0] = _sigmoid(z[:, D_MODEL:]).astype(BF16)

    zt = _dot_nt(wkvt_ref[...], n_ref[...])
    cost, sint = cost_ref[...], sint_ref[...]
    chunks = [slice(j * tk, (j + 1) * tk) for j in range(tm // tk)]
    norms = []
    for s in range(n_q):
        rows = slice(s * LANES, (s + 1) * LANES)
        xk = _rope_t(zt[rows], cost, sint)
        kt_ref[0, rows, :] = xk
        xkb = xk.astype(BF16)
        for j, ch in enumerate(chunks):
            kbt_ref[0, j, rows, :] = xkb[:, ch]
        sq = xkb.astype(F32)
        sq = sq * sq
        norms += [jnp.sum(sq[:HEAD_DIM], axis=0, keepdims=True), jnp.sum(sq[HEAD_DIM:], axis=0, keepdims=True)]
    kn_ref[0] = jnp.concatenate(norms, axis=0)
    zv = zt[ATT_DIM:2 * ATT_DIM]
    vt_ref[0] = zv
    for j, ch in enumerate(chunks):
        vbt_ref[0, j] = zv[:, ch].astype(BF16)
    kid = _rope_t(zt[2 * ATT_DIM:], cost, sint)
    kit_ref[0] = kid[:IDX_DIM]
    for j, ch in enumerate(chunks):
        kidt_ref[0, j] = kid[:, ch].astype(BF16)


def _inproj(h, shift, scale, g, cos, sin, cost, sint, ws):
    s, t, _ = h.shape
    tm = min(TOKEN_TILE, t)
    tk = DSA_TK
    per_token = shift.shape[1] != 1
    mod = _tile_spec(tm, D_MODEL, per_token)
    tab = pl.BlockSpec((tm, LANES), lambda b, i: (i, 0))
    tabt = pl.BlockSpec((LANES, tm), lambda b, i: (0, i))
    tok = [(D_CONV, F32), (ATT_DIM, BF16), (IDX_HEADS * IDX_DIM, BF16), (IDX_HEADS, F32),
           (D_MODEL, BF16), (D_MODEL, BF16)]
    feat = lambda rows, dt: (jax.ShapeDtypeStruct((s, rows, t), dt),
                             pl.BlockSpec((1, rows, tm), lambda b, i: (b, 0, i)))
    chunked = lambda rows: (jax.ShapeDtypeStruct((s, t // tk, rows, tk), BF16),
                            pl.BlockSpec((1, tm // tk, rows, tk), lambda b, i: (b, i, 0, 0)))
    key_side = [feat(ATT_DIM, F32), chunked(ATT_DIM), feat(ATT_DIM, F32), chunked(ATT_DIM),
                chunked(LANES), feat(IDX_DIM, F32), feat(N_HEADS, F32)]
    return pl.pallas_call(
        functools.partial(_inproj_kernel, tm=tm, tk=tk),
        out_shape=[jax.ShapeDtypeStruct((s, t, w), dt) for w, dt in tok] + [sd for sd, _ in key_side],
        grid=(s, t // tm),
        in_specs=[_tile_spec(tm, D_MODEL, True), mod, mod, _const_spec((1, D_MODEL)), tab, tab, tabt, tabt]
                 + [_const_spec(w.shape) for w in ws],
        out_specs=[_tile_spec(tm, w, True) for w, _ in tok] + [sp for _, sp in key_side],
        scratch_shapes=[pltpu.VMEM((tm, D_MODEL), BF16)],
        compiler_params=_params(2),
        name="inproj",
    )(h, shift, scale, g, cos, sin, cost, sint, *ws)


def _ln_silu(y, g, b):
    mu = jnp.mean(y, axis=-1, keepdims=True)
    yc = y - mu
    var = jnp.mean(yc * yc, axis=-1, keepdims=True)
    y = yc * lax.rsqrt(var + EPS) * g + b
    return y * _sigmoid(y)


def _conv_kernel(u_ref, up_ref, wdw_ref, bdw_ref, lng_ref, lnb_ref, wco_ref, o_ref, ext_ref, sh_ref, y_ref, *, tm):
    i = pl.program_id(1)
    ext_ref[0:CONV_HALO, :] = jnp.where(i > 0, up_ref[0], 0.0)
    ext_ref[CONV_HALO:, :] = u_ref[0]
    n_sh = tm + CONV_HALO - SUBLANES
    for s in range(1, SUBLANES):
        sh_ref[s - 1] = ext_ref[s:s + n_sh, :]
    off = CONV_HALO - (CONV_W - 1)
    for r in range(tm // CONV_ROWS):
        acc = jnp.broadcast_to(bdw_ref[...], (CONV_ROWS, D_CONV))
        for j in range(CONV_W):
            s = (off + j) % SUBLANES
            lo = r * CONV_ROWS + off + j - s
            x = ext_ref[lo:lo + CONV_ROWS, :] if s == 0 else sh_ref[s - 1, lo:lo + CONV_ROWS, :]
            acc = acc + wdw_ref[j:j + 1, :] * x
        y_ref[r * CONV_ROWS:(r + 1) * CONV_ROWS, :] = _ln_silu(acc, lng_ref[...], lnb_ref[...]).astype(BF16)
    o_ref[0] = _dot(y_ref[...], wco_ref[...]).astype(BF16)


def _conv_prompt(u, wdw, bdw, lng, lnb, wco):
    b, t, _ = u.shape
    tm = min(TOKEN_TILE, t)
    halo_blocks = tm // CONV_HALO
    return pl.pallas_call(
        functools.partial(_conv_kernel, tm=tm),
        out_shape=jax.ShapeDtypeStruct((b, t, D_MODEL), BF16),
        grid=(b, t // tm),
        in_specs=[pl.BlockSpec((1, tm, D_CONV), lambda s, i: (s, i, 0)),
                  pl.BlockSpec((1, CONV_HALO, D_CONV), lambda s, i: (s, jnp.maximum(i * halo_blocks - 1, 0), 0)),
                  _const_spec(wdw.shape), _const_spec(bdw.shape), _const_spec(lng.shape), _const_spec(lnb.shape),
                  _const_spec(wco.shape)],
        out_specs=pl.BlockSpec((1, tm, D_MODEL), lambda s, i: (s, i, 0)),
        scratch_shapes=[pltpu.VMEM((tm + CONV_HALO, D_CONV), F32),
                        pltpu.VMEM((SUBLANES - 1, tm + CONV_HALO - SUBLANES, D_CONV), F32),
                        pltpu.VMEM((tm, D_CONV), BF16)],
        compiler_params=_params(2),
        name="conv_prompt",
    )(u, u, wdw, bdw, lng, lnb, wco)


def _conv_sample_kernel(ext_ref, wdw_ref, bdw_ref, lng_ref, lnb_ref, wco_ref, o_ref, *, steps, rows):
    for s in range(steps):
        acc = jnp.broadcast_to(bdw_ref[...], (rows, D_CONV))
        for j in range(CONV_W):
            acc = acc + wdw_ref[j:j + 1, :] * ext_ref[s + j]
        y = _ln_silu(acc, lng_ref[...], lnb_ref[...]).astype(BF16)
        o_ref[s] = _dot(y, wco_ref[...]).astype(BF16)


def _conv_sample(ext_t, wdw, bdw, lng, lnb, wco):
    n, rows, _ = ext_t.shape
    steps = n - (CONV_W - 1)
    return pl.pallas_call(
        functools.partial(_conv_sample_kernel, steps=steps, rows=rows),
        out_shape=jax.ShapeDtypeStruct((steps, rows, D_MODEL), BF16),
        grid=(1,),
        in_specs=[_const_spec(ext_t.shape), _const_spec(wdw.shape), _const_spec(bdw.shape),
                  _const_spec(lng.shape), _const_spec(lnb.shape), _const_spec(wco.shape)],
        out_specs=pl.BlockSpec((steps, rows, D_MODEL), lambda i: (0, 0, 0)),
        compiler_params=_params(1),
        name="conv_sample",
    )(ext_t, wdw, bdw, lng, lnb, wco)


def _count_term(kind, scores, kpos, thr, pos):
    if kind == "ge":
        return jnp.where(scores >= thr, 1, 0)
    if kind == "gt":
        return jnp.where(scores > thr, 1, 0)
    if kind == "eq":
        return jnp.where(scores == thr, 1, 0)
    return jnp.where(scores == thr, jnp.where(kpos < pos, 1, 0), 0)


def _select_threshold(count, shape, bits=1):
    assert 32 % bits == 0

    def pass_body(p, key):
        shift = 32 - bits * (p + 1)
        digit = jnp.zeros(shape, I32)
        for i in range(1, 2 ** bits):
            cand = key + lax.shift_left(jnp.int32(i), shift)
            digit = digit + jnp.where(count("ge", _key_to_f32(cand), None) >= TOPK_MAX, 1, 0)
        return key + lax.shift_left(digit, shift)

    key = lax.fori_loop(0, 32 // bits, pass_body, jnp.full(shape, INT_MIN, I32))
    thr = jnp.where(key < KEY_NEG_INF, -jnp.inf, _key_to_f32(key))
    need = TOPK_MAX - count("gt", thr, None)
    n_eq = count("eq", thr, None)
    return thr, need, n_eq


def _tie_position(count, thr, need, n_bits):
    def bit_body(b, j):
        cand = j + lax.shift_left(jnp.int32(1), n_bits - 1 - b)
        return jnp.where(count("tie", thr, cand) < need, cand, j)

    return lax.fori_loop(0, n_bits, bit_body, jnp.zeros(thr.shape, I32))


def _bias_from_scores(scores, kpos, qpos, thr, last_tie):
    t1 = jnp.where(kpos <= last_tie, 0.0, NEG)
    t2 = jnp.where(scores == thr, t1, NEG)
    t3 = jnp.where(scores > thr, 0.0, t2)
    return jnp.where(kpos <= qpos, t3, NEG)


def _split_heads(x, lo):
    zero = jnp.zeros_like(x)
    return jnp.where(lo, x, zero), jnp.where(lo, zero, x)


def _dsa_prompt_kernel(qi_ref, wi_ref, q_ref, kid_ref, kb_ref, vb_ref, kn_ref, qin_ref, win_ref, o_ref,
                       qim_ref, qm_ref, wib_ref, sc_ref, s8_ref, s_ref, p_ref, cand_ref, candt_ref, drop_ref, sel_ref,
                       thr_ref, pos_ref, tie_ref, shift_ref, m_ref, l_ref, al_ref, acc_ref, *, tq, tk, n_bits):
    st = DSA_STRIP
    n_st = tq // st
    n_lt = tk // LANES
    blk = pl.program_id(1)
    slot = blk % 2
    has_next = blk + 1 < pl.num_programs(1)
    q0 = blk * tq
    n_ch = (q0 + tq + tk - 1) // tk
    n_ch_next = (q0 + 2 * tq + tk - 1) // tk
    lo = lax.broadcasted_iota(I32, (tq, LANES), 1) < HEAD_DIM
    strips = [slice(r * st, (r + 1) * st) for r in range(n_st)]
    tiles = [slice(t * LANES, (t + 1) * LANES) for t in range(n_lt)]
    lane_s = lax.broadcasted_iota(I32, (st, LANES), 1)
    row_s = lax.broadcasted_iota(I32, (st, LANES), 0)

    def prepare_scoring(qi_blk, wi_blk):
        for pr in range(N_HEADS // 2):
            sl = slice(pr * LANES, (pr + 1) * LANES)
            qim_ref[2 * pr], qim_ref[2 * pr + 1] = _split_heads(qi_blk[0, :, sl], lo)
        for h in range(IDX_HEADS):
            wib_ref[h] = jnp.broadcast_to(wi_blk[0, :, h:h + 1], (tq, LANES))
        cand_ref[...] = jnp.full(cand_ref.shape, -jnp.inf, F32)
        drop_ref[...] = jnp.full(drop_ref.shape, -jnp.inf, F32)

    def score_chunk(c, dst, qbase):
        kc = kid_ref[0, c]
        for h in range(IDX_HEADS):
            s8_ref[h] = _dot(qim_ref[h], kc)
        for r, rows in enumerate(strips):
            acc = [jnp.zeros((st, LANES), F32) for _ in tiles]
            for h in range(IDX_HEADS):
                w = wib_ref[h, rows, :]
                for t, tl in enumerate(tiles):
                    acc[t] = acc[t] + w * jnp.maximum(s8_ref[h, rows, tl], 0.0)
            xs = []
            for t, tl in enumerate(tiles):
                visible = lane_s + (c * tk + t * LANES) <= row_s + (qbase + r * st)
                xs.append(jnp.where(visible, acc[t], NEG))
                sc_ref[dst, c, rows, tl] = xs[t]
            for j in range(DSA_CANDS):
                kept = cand_ref[j, rows, :]
                for t in range(n_lt):
                    kept, xs[t] = jnp.maximum(kept, xs[t]), jnp.minimum(kept, xs[t])
                cand_ref[j, rows, :] = kept
            dropped = drop_ref[rows, :]
            for x in xs:
                dropped = jnp.maximum(dropped, x)
            drop_ref[rows, :] = dropped

    def score_next(c, carry):
        score_chunk(c, 1 - slot, q0 + tq)
        return carry

    @pl.when(blk == 0)
    def _():
        prepare_scoring(qi_ref, wi_ref)

        def score_first(c, carry):
            score_chunk(c, slot, q0)
            return carry

        lax.fori_loop(0, n_ch, score_first, 0)

    for pr in range(N_HEADS // 2):
        sl = slice(pr * LANES, (pr + 1) * LANES)
        qm_ref[pr, 0:tq], qm_ref[pr, tq:2 * tq] = _split_heads(q_ref[0, :, sl], lo)

    def count(kind, thr, pos):
        thr_ref[...] = jnp.broadcast_to(thr, (tq, LANES))
        if pos is not None:
            pos_ref[...] = jnp.broadcast_to(pos, (tq, LANES))

        def body(c, accs):
            out = []
            for r, rows in enumerate(strips):
                thr_b = thr_ref[rows, :]
                pos_b = pos_ref[rows, :] if pos is not None else None
                a = accs[r]
                for t, tl in enumerate(tiles):
                    kpos = lane_s + (c * tk + t * LANES)
                    a = a + _count_term(kind, sc_ref[slot, c, rows, tl], kpos, thr_b, pos_b)
                out.append(a)
            return tuple(out)

        accs = lax.fori_loop(0, n_ch, body, tuple(jnp.zeros((st, LANES), I32) for _ in strips))
        return jnp.concatenate([jnp.sum(a.astype(F32), axis=1, keepdims=True) for a in accs], axis=0)

    for j in range(DSA_CANDS):
        candt_ref[j * LANES:(j + 1) * LANES, :] = cand_ref[j].T

    def count_kept(kind, thr, pos):
        thr_b = jnp.broadcast_to(thr, (8, tq))
        accs = [jnp.zeros((8, tq), I32) for _ in range(4)]
        for g in range(DSA_CANDS * LANES // 8):
            term = _count_term(kind, candt_ref[g * 8:(g + 1) * 8, :], None, thr_b, None)
            accs[g % 4] = accs[g % 4] + term
        total = (accs[0] + accs[1]) + (accs[2] + accs[3])
        return jnp.sum(total.astype(F32), axis=0, keepdims=True)

    def to_cols(x):
        eye = lax.broadcasted_iota(I32, (LANES, LANES), 0) == lax.broadcasted_iota(I32, (LANES, LANES), 1)
        blocks = []
        for b in range(tq // LANES):
            xb = jnp.broadcast_to(x[:, b * LANES:(b + 1) * LANES], (LANES, LANES))
            col = jnp.sum(jnp.where(eye, xb, 0.0), axis=1, keepdims=True)
            blocks.append(jnp.broadcast_to(col, (LANES, LANES)))
        return jnp.concatenate(blocks, axis=0)

    for v, x in enumerate(_select_threshold(count_kept, (1, tq))):
        sel_ref[v] = to_cols(x)
    dropped = drop_ref[...]
    unsafe = jnp.where(dropped > -jnp.inf, jnp.where(dropped >= sel_ref[0], 1.0, 0.0), 0.0)

    @pl.when(jnp.max(unsafe) > 0.5)
    def _():
        for v, x in enumerate(_select_threshold(count, (tq, 1))):
            sel_ref[v] = jnp.broadcast_to(x, (tq, LANES))

    thr, need, n_eq = (sel_ref[v, :, 0:1] for v in range(3))
    tie_ref[...] = jnp.full((tq, LANES), INT_MAX, I32)
    split = n_eq > need

    @pl.when(jnp.max(jnp.where(split, 1.0, 0.0)) > 0.5)
    def _():
        j = _tie_position(count, thr, need, n_bits)
        tie_ref[...] = jnp.broadcast_to(jnp.where(split, j, INT_MAX), (tq, LANES))

    thr_ref[...] = jnp.broadcast_to(thr, (tq, LANES))

    def to_bias(c, carry):
        for r, rows in enumerate(strips):
            thr_b, tie_b = thr_ref[rows, :], tie_ref[rows, :]
            qpos = row_s + (q0 + r * st)
            for t, tl in enumerate(tiles):
                kpos = lane_s + (c * tk + t * LANES)
                sc_ref[slot, c, rows, tl] = _bias_from_scores(sc_ref[slot, c, rows, tl], kpos, qpos, thr_b, tie_b)
        return carry

    lax.fori_loop(0, n_ch, to_bias, 0)

    @pl.when(has_next)
    def _():
        prepare_scoring(qin_ref, win_ref)

    l_ref[...] = jnp.zeros(l_ref.shape, F32)
    acc_ref[...] = jnp.zeros(acc_ref.shape, F32)

    kn = kn_ref[0]
    visible = lax.broadcasted_iota(I32, kn.shape, 1) < q0 + tq
    kmax = jnp.sqrt(jnp.max(jnp.where(visible, kn, 0.0), axis=1, keepdims=True))
    for h in range(N_HEADS):
        qh = qm_ref[h // 2, (h % 2) * tq:(h % 2 + 1) * tq, :].astype(F32)
        qn = jnp.sqrt(jnp.sum(qh * qh, axis=1, keepdims=True))
        shift_ref[h] = jnp.broadcast_to(qn * kmax[h:h + 1, :], (tq, LANES))
    bounded = jnp.max(shift_ref[...]) <= SHIFT_LIMIT

    def attend_chunk_bounded(c, carry):
        for pr in range(N_HEADS // 2):
            feat = slice(pr * LANES, (pr + 1) * LANES)
            s_ref[pr % 2] = _dot(qm_ref[pr], kb_ref[0, c, feat, :])
            for e in range(2):
                h = 2 * pr + e
                for r, rows in enumerate(strips):
                    srows = slice(e * tq + r * st, e * tq + (r + 1) * st)
                    shift = shift_ref[h, rows, :]
                    psum = l_ref[h, rows, :]
                    for tl in tiles:
                        p = jnp.exp2(s_ref[pr % 2, srows, tl] + sc_ref[slot, c, rows, tl] - shift)
                        psum = psum + p
                        p_ref[pr % 2, srows, tl] = p.astype(BF16)
                    l_ref[h, rows, :] = psum
            pv = _dot_nt(p_ref[pr % 2], vb_ref[0, c, feat, :])
            acc_ref[pr] = acc_ref[pr] + jnp.where(lo, pv[:tq], pv[tq:])
        return carry

    def attend_and_score_next(c, carry):
        attend_chunk_bounded(c, carry)
        return score_next(c, carry)

    @pl.when(jnp.logical_and(bounded, has_next))
    def _():
        lax.fori_loop(0, n_ch, attend_and_score_next, 0)

    @pl.when(jnp.logical_and(bounded, jnp.logical_not(has_next)))
    def _():
        lax.fori_loop(0, n_ch, attend_chunk_bounded, 0)

    def attend_chunk(c, carry):
        for pr in range(N_HEADS // 2):
            feat = slice(pr * LANES, (pr + 1) * LANES)
            s_ref[pr % 2] = _dot(qm_ref[pr], kb_ref[0, c, feat, :])
            for e in range(2):
                h = 2 * pr + e
                for r, rows in enumerate(strips):
                    srows = slice(e * tq + r * st, e * tq + (r + 1) * st)
                    s = [s_ref[pr % 2, srows, tl] + sc_ref[slot, c, rows, tl] for tl in tiles]
                    mx = s[0]
                    for x in s[1:]:
                        mx = jnp.maximum(mx, x)
                    m_prev = m_ref[h, rows, :]
                    m_new = jnp.maximum(m_prev, jnp.max(mx, axis=1, keepdims=True))
                    alpha = jnp.exp2(m_prev - m_new)
                    p = [jnp.exp2(x - m_new) for x in s]
                    psum = p[0]
                    for x in p[1:]:
                        psum = psum + x
                    l_ref[h, rows, :] = alpha * l_ref[h, rows, :] + psum
                    m_ref[h, rows, :] = m_new
                    al_ref[e, rows, :] = alpha
                    for x, tl in zip(p, tiles):
                        p_ref[pr % 2, srows, tl] = x.astype(BF16)
            pv = _dot_nt(p_ref[pr % 2], vb_ref[0, c, feat, :])
            a = acc_ref[pr]
            acc_ref[pr] = jnp.where(lo, a * al_ref[0] + pv[:tq], a * al_ref[1] + pv[tq:])
        return carry

    @pl.when(jnp.logical_not(bounded))
    def _():
        m_ref[...] = jnp.full(m_ref.shape, NEG, F32)
        lax.fori_loop(0, n_ch, attend_chunk, 0)

    @pl.when(has_next)
    def _():
        first = jnp.where(bounded, n_ch, 0)
        lax.fori_loop(first, n_ch_next, score_next, 0)

    for pr in range(N_HEADS // 2):
        l_e = jnp.sum(l_ref[2 * pr], axis=1, keepdims=True)
        l_o = jnp.sum(l_ref[2 * pr + 1], axis=1, keepdims=True)
        o_ref[0, :, pr * LANES:(pr + 1) * LANES] = (acc_ref[pr] * jnp.where(lo, 1.0 / l_e, 1.0 / l_o)).astype(BF16)


def _dsa_prompt(qi, wi, q, kidt, kbt, vbt, knorm):
    b, t, _ = q.shape
    tq, tk = DSA_TQ, DSA_TK
    assert tk % tq == 0 and t % tk == 0
    n_bits = max(1, (t - 1).bit_length())
    last = t // tq - 1
    qspec = lambda w: pl.BlockSpec((1, tq, w), lambda s, i: (s, i, 0))
    qnext = lambda w: pl.BlockSpec((1, tq, w), lambda s, i: (s, jnp.minimum(i + 1, last), 0))
    kspec = lambda rows: pl.BlockSpec((1, t // tk, rows, tk), lambda s, i: (s, 0, 0, 0),
                                      pipeline_mode=pl.Buffered(1))
    return pl.pallas_call(
        functools.partial(_dsa_prompt_kernel, tq=tq, tk=tk, n_bits=n_bits),
        out_shape=jax.ShapeDtypeStruct((b, t, ATT_DIM), BF16),
        grid=(b, t // tq),
        in_specs=[qspec(IDX_HEADS * IDX_DIM), qspec(IDX_HEADS), qspec(ATT_DIM),
                  kspec(LANES), kspec(ATT_DIM), kspec(ATT_DIM),
                  pl.BlockSpec((1, N_HEADS, t), lambda s, i: (s, 0, 0), pipeline_mode=pl.Buffered(1)),
                  qnext(IDX_HEADS * IDX_DIM), qnext(IDX_HEADS)],
        out_specs=qspec(ATT_DIM),
        scratch_shapes=[pltpu.VMEM((IDX_HEADS, tq, LANES), BF16),
                        pltpu.VMEM((N_HEADS // 2, 2 * tq, LANES), BF16),
                        pltpu.VMEM((IDX_HEADS, tq, LANES), F32),
                        pltpu.VMEM((2, t // tk, tq, tk), F32),
                        pltpu.VMEM((IDX_HEADS, tq, tk), F32),
                        pltpu.VMEM((2, 2 * tq, tk), F32),
                        pltpu.VMEM((2, 2 * tq, tk), BF16),
                        pltpu.VMEM((DSA_CANDS, tq, LANES), F32),
                        pltpu.VMEM((DSA_CANDS * LANES, tq), F32),
                        pltpu.VMEM((tq, LANES), F32),
                        pltpu.VMEM((3, tq, LANES), F32),
                        pltpu.VMEM((tq, LANES), F32),
                        pltpu.VMEM((tq, LANES), I32),
                        pltpu.VMEM((tq, LANES), I32),
                        pltpu.VMEM((N_HEADS, tq, LANES), F32),
                        pltpu.VMEM((N_HEADS, tq, LANES), F32),
                        pltpu.VMEM((N_HEADS, tq, LANES), F32),
                        pltpu.VMEM((2, tq, LANES), F32),
                        pltpu.VMEM((N_HEADS // 2, tq, LANES), F32)],
        compiler_params=_params(2),
        name="dsa_prompt",
    )(qi, wi, q, kidt, kbt, vbt, knorm, qi, wi)


def _dsa_sample_kernel(pt_ref, qi_ref, wi_ref, q_ref, kidn_ref, kn_ref, vn_ref, *rest, n_pages, rows, past, group):
    n_pg = group * n_pages
    idx_refs = rest[:n_pg]
    k_refs = rest[n_pg:2 * n_pg]
    v_refs = rest[2 * n_pg:3 * n_pg]
    o_ref, kid_all, k_all, v_all, tie_ref = rest[3 * n_pg:]
    n_keys = past + PAGE_SIZE
    tail = slice(past, n_keys)
    lo = lax.broadcasted_iota(I32, (rows, LANES), 1) < HEAD_DIM

    accs = []
    for g in range(group):
        for p in range(n_pages):
            sl = slice(p * PAGE_SIZE, (p + 1) * PAGE_SIZE)
            x = idx_refs[g * n_pages + p][0]
            kid_all[g, :, sl] = jnp.concatenate([x, x], axis=0).astype(BF16)
            k_all[g, :, sl] = k_refs[g * n_pages + p][0].astype(BF16)
            v_all[g, :, sl] = v_refs[g * n_pages + p][0].astype(BF16)
        kid_all[g, :, tail] = kidn_ref[g]
        k_all[g, :, tail] = kn_ref[g]
        v_all[g, :, tail] = vn_ref[g]

        qi = qi_ref[g]
        parts = []
        for pr in range(IDX_HEADS // 2):
            parts.extend(_split_heads(qi[:, pr * LANES:(pr + 1) * LANES], lo))
        s_all = _dot(jnp.concatenate(parts, axis=0).astype(BF16), kid_all[g])
        wi = wi_ref[g]
        acc = jnp.zeros((rows, n_keys), F32)
        for h in range(IDX_HEADS):
            acc = acc + wi[:, h:h + 1] * jnp.maximum(s_all[h * rows:(h + 1) * rows], 0.0)
        accs.append(acc)

    n_q = group * rows
    kpos = lax.broadcasted_iota(I32, (n_q, n_keys), 1)
    qpos = (lax.broadcasted_iota(I32, (n_q, n_keys), 0) & (rows - 1)) + past
    scores = jnp.where(kpos <= qpos, jnp.concatenate(accs, axis=0), NEG)

    def count(kind, thr, pos):
        return jnp.sum(_count_term(kind, scores, kpos, thr, pos).astype(F32), axis=1, keepdims=True)

    n_bits = (n_keys - 1).bit_length()
    thr, need, n_eq = _select_threshold(count, (n_q, 1), DECODE_BITS)
    tie_ref[...] = jnp.full((n_q, 1), INT_MAX, I32)
    split = n_eq > need

    @pl.when(jnp.max(jnp.where(split, 1.0, 0.0)) > 0.5)
    def _():
        tie_ref[...] = jnp.where(split, _tie_position(count, thr, need, n_bits), INT_MAX)

    bias = _bias_from_scores(scores, kpos, qpos, thr, tie_ref[...])

    logits = []
    for g in range(group):
        bias_g = bias[g * rows:(g + 1) * rows]
        bias2 = jnp.concatenate([bias_g, bias_g], axis=0)
        q = q_ref[g]
        for pr in range(N_HEADS // 2):
            sl = slice(pr * LANES, (pr + 1) * LANES)
            lhs = jnp.concatenate(_split_heads(q[:, sl], lo), axis=0).astype(BF16)
            logits.append(_dot(lhs, k_all[g, sl, :]) + bias2)
    s = jnp.concatenate(logits, axis=0)
    p = jnp.exp2(s - jnp.max(s, axis=1, keepdims=True))
    l = jnp.sum(p, axis=1, keepdims=True)
    p = p.astype(BF16)
    for g in range(group):
        for pr in range(N_HEADS // 2):
            sl = slice(pr * LANES, (pr + 1) * LANES)
            blk = slice((g * (N_HEADS // 2) + pr) * 2 * rows, (g * (N_HEADS // 2) + pr + 1) * 2 * rows)
            pv = _dot_nt(p[blk], v_all[g, sl, :]) / l[blk]
            o_ref[g, :, sl] = jnp.where(lo, pv[:rows], pv[rows:])


def _dsa_sample(page_table, qi, wi, q, kidn, kn, vn, cache_idx, cache_k, cache_v):
    db, rows, _ = q.shape
    n_pages = page_table.shape[1]
    past = n_pages * PAGE_SIZE
    n_keys = past + PAGE_SIZE
    group = DECODE_GROUP if db % DECODE_GROUP == 0 else 1
    n_pg = group * n_pages
    assert rows & (rows - 1) == 0
    seq = lambda w: pl.BlockSpec((group, rows, w), lambda s, pt: (s, 0, 0))
    new = lambda f: pl.BlockSpec((group, f, PAGE_SIZE), lambda s, pt: (s, 0, 0))

    def page(f, gp):
        return pl.BlockSpec((1, f, PAGE_SIZE), lambda s, pt: (pt[s * n_pg + gp], 0, 0))

    grid_spec = pltpu.PrefetchScalarGridSpec(
        num_scalar_prefetch=1,
        grid=(db // group,),
        in_specs=[seq(IDX_HEADS * IDX_DIM), seq(IDX_HEADS), seq(ATT_DIM), new(LANES), new(ATT_DIM), new(ATT_DIM)]
                 + [page(IDX_DIM, gp) for gp in range(n_pg)]
                 + [page(ATT_DIM, gp) for gp in range(n_pg)]
                 + [page(ATT_DIM, gp) for gp in range(n_pg)],
        out_specs=seq(ATT_DIM),
        scratch_shapes=[pltpu.VMEM((group, LANES, n_keys), BF16), pltpu.VMEM((group, ATT_DIM, n_keys), BF16),
                        pltpu.VMEM((group, ATT_DIM, n_keys), BF16), pltpu.VMEM((group * rows, 1), I32)],
    )
    return pl.pallas_call(
        functools.partial(_dsa_sample_kernel, n_pages=n_pages, rows=rows, past=past, group=group),
        out_shape=jax.ShapeDtypeStruct((db, rows, ATT_DIM), F32),
        grid_spec=grid_spec,
        compiler_params=_params(1),
        name="dsa_sample",
    )(page_table.reshape(-1), qi, wi, q, kidn, kn, vn,
      *([cache_idx] * n_pg), *([cache_k] * n_pg), *([cache_v] * n_pg))


def _out_kernel(h_ref, att_ref, conv_ref, sgc_ref, sga_ref, g2_ref, sh_ref, sc_ref, g3_ref, gn_ref, gfin_ref,
                wao_ref, wo_ref, wg_ref, wu_ref, wd_ref, y_ref, hn_ref, acc_ref):
    att_o = _dot(att_ref[0], wao_ref[...])
    mix = sgc_ref[0].astype(F32) * conv_ref[0].astype(F32) + sga_ref[0].astype(F32) * att_o
    h2 = h_ref[0] + g2_ref[0] * _dot(mix.astype(BF16), wo_ref[...])
    y_ref[0] = h2
    hn_ref[...] = _rms_mod(h2, gn_ref[...], sc_ref[0], sh_ref[0]).astype(BF16)
    _swiglu_into(hn_ref, wg_ref, wu_ref, wd_ref, acc_ref)
    h3 = y_ref[0] + 0.5 * g3_ref[0] * acc_ref[...]
    ms = jnp.mean(h3 * h3, axis=-1, keepdims=True)
    y_ref[0] = h3 * lax.rsqrt(ms + EPS) * gfin_ref[...]


def _out(h, att, conv, sgc, sga, g2, shift, scale, g3, gn, gfin, wao, wo, wg, wu, wd):
    s, t, _ = h.shape
    tm = min(TOKEN_TILE, t)
    per_token = shift.shape[1] != 1
    mod = _tile_spec(tm, D_MODEL, per_token)
    tok = lambda w: _tile_spec(tm, w, True)
    return pl.pallas_call(
        _out_kernel,
        out_shape=jax.ShapeDtypeStruct(h.shape, F32),
        grid=(s, t // tm),
        in_specs=[tok(D_MODEL), tok(ATT_DIM), tok(D_MODEL), tok(D_MODEL), tok(D_MODEL), mod, mod, mod, mod,
                  _const_spec((1, D_MODEL)), _const_spec((1, D_MODEL)),
                  _const_spec(wao.shape), _const_spec(wo.shape),
                  _const_spec(wg.shape), _const_spec(wu.shape), _const_spec(wd.shape)],
        out_specs=tok(D_MODEL),
        scratch_shapes=[pltpu.VMEM((tm, D_MODEL), BF16), pltpu.VMEM((tm, D_MODEL), F32)],
        compiler_params=_params(2),
        name="out_ffn2",
    )(h, att, conv, sgc, sga, g2, shift, scale, g3, gn, gfin, wao, wo, wg, wu, wd)


def _rope_tables(pos):
    inv = ROPE_THETA ** (-jnp.arange(0, HEAD_DIM, 2, dtype=F32) / HEAD_DIM)
    ang = pos.astype(F32)[:, None] * inv[None, :]
    cos, sin = jnp.cos(ang), jnp.sin(ang)
    cos = jnp.concatenate([cos] * 4, axis=1)
    sin = jnp.concatenate([-sin, sin, -sin, sin], axis=1)
    return cos, sin, cos.T, sin.T


def _ff_weights(wg, wu, wd):
    split_cols = lambda w: w.reshape(D_MODEL, N_FF_CHUNKS, FF_CHUNK).transpose(1, 0, 2).astype(BF16)
    return split_cols(wg), split_cols(wu), wd.reshape(N_FF_CHUNKS, FF_CHUNK, D_MODEL).astype(BF16)


def _in_weights(w_in):
    cuts = [0]
    for width in (D_CONV, D_CONV, ATT_DIM, ATT_DIM, ATT_DIM, IDX_HEADS * IDX_DIM, IDX_DIM, IDX_HEADS,
                  D_MODEL, D_MODEL):
        cuts.append(cuts[-1] + width)
    piece = lambda a, b: w_in[:, cuts[a]:cuts[b]]
    w_qw = jnp.concatenate([piece(5, 6), piece(7, 8), jnp.zeros((D_MODEL, LANES - IDX_HEADS), F32)], axis=1)
    w_ki = piece(6, 7)
    w_kvt = jnp.concatenate([piece(3, 5), w_ki, w_ki], axis=1).T
    return [w.astype(BF16) for w in (piece(0, 2), piece(2, 3), w_qw, piece(8, 10), w_kvt)]


def kernel(x_prompt, x_sample, cache_k, cache_v, cache_idx_k, state_conv, page_table, c_prompt, c_sample,
           w_ada, b_ada, g_ffn1, w1_gate, w1_up, w1_down, g_mix, w_in, w_dw, b_dw, ln_g, ln_b,
           w_conv_out, w_attn_o, w_out, g_ffn2, w2_gate, w2_up, w2_down, g_final):
    bsz, t, _ = x_prompt.shape
    db, s, _ = x_sample.shape
    depth = w_ada.shape[0]
    n_pool = cache_k.shape[1]
    past = page_table.shape[1] * PAGE_SIZE
    q_rows = 8
    assert depth == 1 and t >= CONV_W - 1 and s <= q_rows

    rope_p = _rope_tables(jnp.arange(t))
    rope_s = _rope_tables(jnp.tile(past + jnp.arange(s), db))
    row = lambda v: v.reshape(1, -1)

    c_all = jnp.concatenate([c_prompt, c_sample], axis=0)
    n_c = c_all.shape[0]
    c_all = jnp.pad(c_all, ((0, -n_c % 8), (0, 0)))

    hp = x_prompt
    hs = x_sample.reshape(1, db * s, D_MODEL)
    outs = [[] for _ in range(8)]
    for l in range(depth):
        m = _ada(c_all, w_ada[l], b_ada[l])
        mod_p = [m[:bsz, i * D_MODEL:(i + 1) * D_MODEL].reshape(bsz, 1, D_MODEL) for i in range(N_MOD)]
        mod_s = [jnp.repeat(m[bsz:n_c, i * D_MODEL:(i + 1) * D_MODEL], s, axis=0).reshape(1, db * s, D_MODEL)
                 for i in range(N_MOD)]
        ff1 = _ff_weights(w1_gate[l], w1_up[l], w1_down[l])
        ff2 = _ff_weights(w2_gate[l], w2_up[l], w2_down[l])
        w_ins = _in_weights(w_in[l])
        conv_w = (w_dw[l], row(b_dw[l]), row(ln_g[l]), row(ln_b[l]), w_conv_out[l].astype(BF16))
        wao, wo = w_attn_o[l].astype(BF16), w_out[l].astype(BF16)

        def front(h, mod, rope):
            h = _ffn(h, mod[0], mod[1], mod[2], row(g_ffn1[l]), *ff1)
            return h, _inproj(h, mod[3], mod[4], row(g_mix[l]), *rope, w_ins)

        def back(h, att, conv, sgc, sga, mod):
            return _out(h, att, conv, sgc, sga, mod[5], mod[6], mod[7], mod[8], row(g_ffn2[l]), row(g_final),
                        wao, wo, *ff2)

        hp, (u, q, qi, wi, sgc, sga, kt, kbt, vt, vbt, kidt, kit, knorm) = front(hp, mod_p, rope_p)
        conv = _conv_prompt(u, *conv_w)
        att = _dsa_prompt(qi, wi, q, kidt, kbt, vbt, knorm)
        hp = back(hp, att, conv, sgc, sga, mod_p)
        heads_last = lambda a: a.reshape(bsz, N_HEADS, HEAD_DIM, t).transpose(0, 3, 1, 2)
        outs[0].append(heads_last(kt))
        outs[1].append(heads_last(vt))
        outs[2].append(kit.transpose(0, 2, 1))
        outs[3].append(u[:, t - (CONV_W - 1):])

        hs, (u, q, qi, wi, sgc, sga, kt, kbt, vt, vbt, kidt, kit, _) = front(hs, mod_s, rope_s)
        per_seq = lambda a: jnp.pad(a.astype(F32).reshape(db, s, -1), ((0, 0), (0, q_rows - s), (0, 0)))

        def new_keys(a):
            f = a.shape[2]
            a = a[0].transpose(1, 0, 2).reshape(f, db, s).transpose(1, 0, 2)
            return jnp.pad(a, ((0, 0), (0, 0), (0, PAGE_SIZE - s)))

        ext_t = jnp.concatenate([state_conv[l].transpose(1, 0, 2), u.reshape(db, s, D_CONV).transpose(1, 0, 2)],
                                axis=0)
        conv = _conv_sample(ext_t, *conv_w).transpose(1, 0, 2).reshape(1, db * s, D_MODEL)
        att = _dsa_sample(page_table, per_seq(qi), per_seq(wi), per_seq(q),
                          new_keys(kidt), new_keys(kbt), new_keys(vbt),
                          cache_idx_k[l].transpose(0, 2, 1),
                          cache_k[l].transpose(0, 2, 3, 1).reshape(n_pool, ATT_DIM, PAGE_SIZE),
                          cache_v[l].transpose(0, 2, 3, 1).reshape(n_pool, ATT_DIM, PAGE_SIZE))
        att = att[:, :s].reshape(1, db * s, ATT_DIM).astype(BF16)
        hs = back(hs, att, conv, sgc, sga, mod_s)
        heads_last = lambda a: a.reshape(N_HEADS, HEAD_DIM, db, s).transpose(2, 3, 0, 1)
        outs[4].append(heads_last(kt))
        outs[5].append(heads_last(vt))
        outs[6].append(kit.reshape(IDX_DIM, db, s).transpose(1, 2, 0))
        outs[7].append(ext_t[s:].transpose(1, 0, 2))

    return (hp, hs.reshape(db, s, D_MODEL)) + tuple(jnp.stack(o) for o in outs)
```
